```python
import math
import jax, jax.numpy as jnp
from jax import lax
import numpy as np

D_MODEL = 1024
BATCH = 8
SEQ = 8192
DEPTH = 4

GRID_W = 64
CTX_LEN = 256
N_MIXERS = 3
N_A = (DEPTH + 2) // 3
N_B = (DEPTH + 1) // 3
N_C = DEPTH // 3
NORM_EPS = 1e-6

RW_HEAD = 64
RW_HEADS = D_MODEL // RW_HEAD
RW_DECAY_LORA = 64
RW_AAA_LORA = 64
RW_GATE_LORA = 128
RW_GN_EPS = 64e-5

GLA_HEADS = 4
GLA_DK = D_MODEL // 2 // GLA_HEADS
GLA_DV = D_MODEL // GLA_HEADS
GLA_GATE_RANK = 16
GLA_GATE_NORM = 16.0
GLA_CHUNK = 64

HY_ORDER = 2
HY_EMB = 33
HY_BANDS = (HY_EMB - 1) // 2
HY_FILTER_W = 64
HY_FAST_DECAY = 0.3
HY_SLOW_DECAY = 1.5
HY_TARGET = 1e-2

PEER_HEADS = 8
PEER_NKEYS = 128
PEER_EXPERTS = PEER_NKEYS * PEER_NKEYS
PEER_QDIM = 256
PEER_HALF = PEER_QDIM // 2
PEER_TOPK = 16
PEER_BLOCK = 128

kernel_name = 'hybrid_rwkv7_gla_hyena_peer_dit'


def rmsnorm(x, g):
    xf = x.astype(jnp.float32)
    y = xf * lax.rsqrt(jnp.mean(xf * xf, -1, keepdims=True) + NORM_EPS)
    return y.astype(x.dtype) * g


def modulate(x, g, shift, scale):
    return rmsnorm(x, g) * (1 + scale) + shift


def shift_grid(h):
    b_, l, d = h.shape
    rows = l // GRID_W
    g = h.reshape(b_, rows, GRID_W, d)
    q = d // 4
    left = jnp.pad(g[:, :, :-1, :q], ((0, 0), (0, 0), (1, 0), (0, 0)))
    right = jnp.pad(g[:, :, 1:, q:2 * q], ((0, 0), (0, 0), (0, 1), (0, 0)))
    up = jnp.pad(g[:, :-1, :, 2 * q:3 * q], ((0, 0), (1, 0), (0, 0), (0, 0)))
    down = jnp.pad(g[:, 1:, :, 3 * q:], ((0, 0), (0, 1), (0, 0), (0, 0)))
    return jnp.concatenate([left, right, up, down], -1).reshape(b_, l, d)


def shift_seq(h):
    d = h.shape[-1] // 2
    prev = jnp.pad(h[:, :-1, :d], ((0, 0), (1, 0), (0, 0)))
    nxt = jnp.pad(h[:, 1:, d:], ((0, 0), (0, 1), (0, 0)))
    return jnp.concatenate([prev, nxt], -1)


def rwkv_scan(r, w, k, v, kk, b, s0, reverse):
    def step(s, inp):
        r_t, w_t, k_t, v_t, kk_t, b_t = inp
        sa = -jnp.einsum('bhvk,bhk->bhv', s, kk_t)
        s = s * w_t[:, :, None, :] + sa[..., None] * b_t[:, :, None, :] + v_t[..., None] * k_t[:, :, None, :]
        return s, jnp.einsum('bhvk,bhk->bhv', s, r_t)
    xs = tuple(jnp.moveaxis(a, 1, 0) for a in (r, w, k, v, kk, b))
    s_fin, ys = lax.scan(step, s0, xs, reverse=reverse)
    return jnp.moveaxis(ys, 0, 1), s_fin


def rwkv_mixer(hc, hl, need_ctx, p):
    (mu, w_rkv, w0, w1, w2, a0, a1, a2, g1, g2, k_k, k_a, r_k, ln_w, ln_b, w_o) = p
    f32 = jnp.float32

    def run(h, shifted, s_init):
        b_, l, d = h.shape
        heads = lambda t: t.reshape(b_, l, RW_HEADS, RW_HEAD).astype(f32)
        xx = shifted - h
        rkv = jnp.einsum('jbld,jde->jble', h[None] + xx[None] * mu[:3, None, None, :], w_rkv)
        r, k, v = heads(rkv[0]), rkv[1], heads(rkv[2])
        xw, xa, xg = h + xx * mu[3], h + xx * mu[4], h + xx * mu[5]
        kk = heads(k * k_k)
        kk = kk * lax.rsqrt(jnp.sum(kk * kk, -1, keepdims=True) + 1e-12)
        y = jnp.zeros_like(v)
        k_sum = jnp.zeros_like(v)
        finals = []
        for dr in range(2):
            wlog = -jax.nn.softplus(-(w0[dr] + jnp.tanh(xw @ w1[dr]) @ w2[dr]).astype(f32)) - 0.5
            a = jax.nn.sigmoid((a0[dr] + (xa @ a1[dr]) @ a2[dr]).astype(f32))
            kd = heads(k * (1 + (a - 1) * k_a))
            ys, sf = rwkv_scan(r, jnp.exp(-jnp.exp(heads(wlog))), kd, v, kk, kk * heads(a), s_init[dr], dr == 1)
            y = y + ys
            k_sum = k_sum + kd
            finals.append(sf)
        return (y, r, v, k_sum, xg), finals

    def post(h, y, r, v, k_sum, xg):
        b_, l, d = h.shape
        mean = jnp.mean(y, -1, keepdims=True)
        var = jnp.mean(jnp.square(y - mean), -1, keepdims=True)
        yn = ((y - mean) * lax.rsqrt(var + RW_GN_EPS)).reshape(b_, l, d) * ln_w + ln_b
        bonus = jnp.sum(r * k_sum * r_k.reshape(RW_HEADS, RW_HEAD).astype(f32), -1, keepdims=True) * v
        g = jax.nn.sigmoid(xg @ g1) @ g2
        return ((yn + bonus.reshape(b_, l, d)).astype(h.dtype) * g) @ w_o

    s_zero = jnp.zeros((hc.shape[0], RW_HEADS, RW_HEAD, RW_HEAD), f32)
    out_c, s_ctx = run(hc, shift_seq(hc), (s_zero, s_zero))
    out_l, _ = run(hl, shift_grid(hl), s_ctx)
    yc = post(hc, *out_c) if need_ctx else None
    return yc, post(hl, *out_l)


def gla_chunked(q, k, v, logd, s0):
    b_, l, h, _ = q.shape
    dv = v.shape[-1]
    c = GLA_CHUNK
    n = l // c
    ch = lambda t: t.reshape(b_, n, c, h, t.shape[-1])
    q, k, v, logd = ch(q), ch(k), ch(v), ch(logd)
    cum = jnp.cumsum(logd, axis=2)
    ref = cum[:, :, c // 2:c // 2 + 1]
    last = cum[:, :, -1:]
    scores = jnp.einsum('bnihd,bnjhd->bnhij', q * jnp.exp(cum - ref), k * jnp.exp(ref - cum))
    scores = jnp.where(jnp.tril(jnp.ones((c, c), bool)), scores, 0.0)
    o_intra = jnp.einsum('bnhij,bnjhv->bnihv', scores, v)
    q_in = q * jnp.exp(cum)
    k_out = k * jnp.exp(last - cum)

    def step(s, inp):
        q_n, k_n, v_n, dec_n = inp
        o = jnp.einsum('bihd,bhdv->bihv', q_n, s)
        s = s * dec_n[..., None] + jnp.einsum('bjhd,bjhv->bhdv', k_n, v_n)
        return s, o

    xs = (jnp.moveaxis(q_in, 1, 0), jnp.moveaxis(k_out, 1, 0), jnp.moveaxis(v, 1, 0),
          jnp.moveaxis(jnp.exp(last[:, :, 0]), 1, 0))
    s_fin, o_inter = lax.scan(step, s0, xs)
    o = o_intra + jnp.moveaxis(o_inter, 0, 1)
    return o.reshape(b_, l, h, dv), s_fin


def gla_mixer(hc, hl, need_ctx, p):
    w_in, wg1, wg2, bg, norm_w, w_o = p
    f32 = jnp.float32
    hk = GLA_HEADS * GLA_DK
    flip = lambda t: jnp.flip(t, 1)

    def run(h, s_init):
        b_, l, d = h.shape
        heads = lambda t, e: t.reshape(b_, l, GLA_HEADS, e).astype(f32)
        q, k, v, og = jnp.split(h @ w_in, [hk, 2 * hk, 2 * hk + d], -1)
        q = heads(q, GLA_DK) * GLA_DK ** -0.5
        k, v = heads(k, GLA_DK), heads(v, GLA_DV)
        logd = [jax.nn.log_sigmoid(heads((h @ wg1[dr]) @ wg2[dr] + bg[dr], GLA_DK)) / GLA_GATE_NORM
                for dr in range(2)]
        o_f, s_f = gla_chunked(q, k, v, logd[0], s_init[0])
        o_b, s_b = gla_chunked(flip(q), flip(k), flip(v), flip(logd[1]), s_init[1])
        return o_f + flip(o_b), og, (s_f, s_b)

    def post(h, o, og):
        b_, l, d = h.shape
        on = o * lax.rsqrt(jnp.mean(o * o, -1, keepdims=True) + NORM_EPS) * norm_w
        return (on.reshape(b_, l, d).astype(h.dtype) * jax.nn.silu(og)) @ w_o

    s_zero = jnp.zeros((hc.shape[0], GLA_HEADS, GLA_DK, GLA_DV), f32)
    oc, ogc, s_ctx = run(hc, (s_zero, s_zero))
    ol, ogl, _ = run(hl, s_ctx)
    yc = post(hc, oc, ogc) if need_ctx else None
    return yc, post(hl, ol, ogl)


def conv3(z, w, b):
    zp = jnp.pad(z, ((0, 0), (1, 1), (0, 0)))
    return zp[:, :-2] * w[0] + zp[:, 1:-1] * w[1] + zp[:, 2:] * w[2] + b


def hyena_filters(l, f1, fb1, f2, fb2, f3, fb3, f4, freq):
    f32 = jnp.float32
    t = jnp.linspace(0.0, 1.0, l, dtype=f32)[:, None]
    w = 2 * math.pi * jnp.arange(l, dtype=f32)[:, None] / l
    f = jnp.linspace(1e-4, HY_BANDS - 1, HY_BANDS, dtype=f32)[None]
    z = jnp.concatenate([t, jnp.cos(f * w), -jnp.sin(f * w)], -1)
    fr = freq.astype(f32)
    hdn = jnp.sin(fr * (z @ f1.astype(f32) + fb1.astype(f32)))
    hdn = jnp.sin(fr * (hdn @ f2.astype(f32) + fb2.astype(f32)))
    hdn = jnp.sin(fr * (hdn @ f3.astype(f32) + fb3.astype(f32)))
    filt = (hdn @ f4.astype(f32)).reshape(l, HY_ORDER, 2, D_MODEL)
    min_decay = math.log(HY_TARGET) / HY_SLOW_DECAY
    max_decay = math.log(HY_TARGET) / HY_FAST_DECAY
    deltas = jnp.linspace(min_decay, max_decay, D_MODEL, dtype=f32)
    filt = filt * jnp.exp(-t * jnp.abs(deltas))[:, None, None, :]
    fwd, bwd = filt[:, :, 0], filt[:, :, 1]
    kc = jnp.concatenate([fwd, jnp.zeros_like(fwd[:1]), jnp.flip(bwd[1:], 0)], 0)
    return kc / jnp.sum(jnp.abs(kc), 0, keepdims=True)


def long_conv(u, kc_o, bias_o):
    l = u.shape[1]
    uf = jnp.fft.rfft(u, n=2 * l, axis=1)
    kf = jnp.fft.rfft(kc_o, n=2 * l, axis=0)
    return jnp.fft.irfft(uf * kf[None], n=2 * l, axis=1)[:, :l] + u * bias_o


def hyena_mixer(h, p):
    (w_in, b_in, sc_w, sc_b, f1, fb1, f2, fb2, f3, fb3, f4, freq, bias, w_o, b_o) = p
    l = h.shape[1]
    z = conv3(h @ w_in + b_in, sc_w, sc_b).astype(jnp.float32)
    v, x1, x2 = jnp.split(z, 3, -1)
    kc = hyena_filters(l, f1, fb1, f2, fb2, f3, fb3, f4, freq)
    bias = bias.astype(jnp.float32)
    y = x1 * long_conv(v, kc[:, 0], bias[0])
    y = x2 * long_conv(y, kc[:, 1], bias[1])
    return y.astype(h.dtype) @ w_o + b_o


def peer(h, p):
    wq, keys, u, v = p
    b_, l, d = h.shape
    blocks = h.reshape(-1, PEER_BLOCK, d)

    def block(hb):
        q = (hb @ wq).reshape(PEER_BLOCK, PEER_HEADS, 2, PEER_HALF)
        s = jnp.einsum('thpc,hpnc->thpn', q, keys)
        sv, si = lax.top_k(s, PEER_TOPK)
        comb = (sv[:, :, 0, :, None] + sv[:, :, 1, None, :]).reshape(PEER_BLOCK, PEER_HEADS, PEER_TOPK * PEER_TOPK)
        fv, fi = lax.top_k(comb, PEER_TOPK)
        i1 = jnp.take_along_axis(si[:, :, 0], fi // PEER_TOPK, axis=-1)
        i2 = jnp.take_along_axis(si[:, :, 1], fi % PEER_TOPK, axis=-1)
        e = i1 * PEER_NKEYS + i2
        g = jax.nn.softmax(fv.astype(jnp.float32), -1).astype(hb.dtype)
        act = jax.nn.gelu(jnp.einsum('td,thkd->thk', hb, jnp.take(u, e, axis=0)), approximate=False) * g
        return jnp.einsum('thk,thkd->td', act, jnp.take(v, e, axis=0))

    return lax.map(block, blocks).reshape(b_, l, d)


def setup_inputs(seed: int = 0) -> dict:
    key = jax.random.key(seed)
    ks = iter(jax.random.split(key, 64))
    nrm = lambda shape, s: s * jax.random.normal(next(ks), shape, jnp.float32)
    uni = lambda shape, lo, hi: jax.random.uniform(next(ks), shape, jnp.float32, lo, hi)
    D = D_MODEL
    s = D ** -0.5
    hk = GLA_HEADS * GLA_DK
    W = HY_FILTER_W
    return {
        'x': nrm((BATCH, SEQ, D), 1.0),
        'c': nrm((BATCH, D), 1.0),
        'ctx': nrm((BATCH, CTX_LEN, D), 1.0),
        'c_ctx': nrm((D,), 1.0),
        'ada_w': nrm((DEPTH, D, 6 * D), 0.5 * s),
        'ada_b': nrm((DEPTH, 6 * D), 0.02),
        'norm_mix': 1.0 + nrm((DEPTH, D), 0.02),
        'norm_ffn': 1.0 + nrm((DEPTH, D), 0.02),
        'norm_out': 1.0 + nrm((D,), 0.02),
        'rw_mu': uni((N_A, 6, D), 0.0, 1.0),
        'rw_wrkv': nrm((N_A, 3, D, D), s),
        'rw_w0': uni((N_A, 2, D), -6.0, -1.0),
        'rw_w1': nrm((N_A, 2, D, RW_DECAY_LORA), s),
        'rw_w2': nrm((N_A, 2, RW_DECAY_LORA, D), 0.1 * RW_DECAY_LORA ** -0.5),
        'rw_a0': nrm((N_A, 2, D), 0.1),
        'rw_a1': nrm((N_A, 2, D, RW_AAA_LORA), s),
        'rw_a2': nrm((N_A, 2, RW_AAA_LORA, D), 0.5 * RW_AAA_LORA ** -0.5),
        'rw_g1': nrm((N_A, D, RW_GATE_LORA), s),
        'rw_g2': nrm((N_A, RW_GATE_LORA, D), RW_GATE_LORA ** -0.5),
        'rw_kk': 0.85 + nrm((N_A, D), 0.05),
        'rw_ka': 1.0 + nrm((N_A, D), 0.05),
        'rw_rk': nrm((N_A, D), 0.1),
        'rw_lnw': 1.0 + nrm((N_A, D), 0.02),
        'rw_lnb': nrm((N_A, D), 0.02),
        'rw_wo': nrm((N_A, D, D), s),
        'gla_win': nrm((N_B, D, 2 * hk + 2 * D), s),
        'gla_wg1': nrm((N_B, 2, D, GLA_GATE_RANK), s),
        'gla_wg2': nrm((N_B, 2, GLA_GATE_RANK, hk), GLA_GATE_RANK ** -0.5),
        'gla_bg': nrm((N_B, 2, hk), 0.1),
        'gla_normw': 1.0 + nrm((N_B, GLA_DV), 0.02),
        'gla_wo': nrm((N_B, D, D), s),
        'hy_win': nrm((N_C, D, 3 * D), s),
        'hy_bin': nrm((N_C, 3 * D), 0.02),
        'hy_scw': nrm((N_C, 3, 3 * D), 3 ** -0.5),
        'hy_scb': nrm((N_C, 3 * D), 0.02),
        'hy_f1': nrm((N_C, HY_EMB, W), HY_EMB ** -0.5),
        'hy_fb1': nrm((N_C, W), 0.1),
        'hy_f2': nrm((N_C, W, W), W ** -0.5),
        'hy_fb2': nrm((N_C, W), 0.1),
        'hy_f3': nrm((N_C, W, W), W ** -0.5),
        'hy_fb3': nrm((N_C, W), 0.1),
        'hy_f4': nrm((N_C, W, HY_ORDER * 2 * D), W ** -0.5),
        'hy_freq': 1.0 + nrm((N_C, W), 0.02),
        'hy_bias': nrm((N_C, HY_ORDER, D), 1.0),
        'hy_wo': nrm((N_C, D, D), s),
        'hy_bo': nrm((N_C, D), 0.02),
        'pe_wq': nrm((DEPTH, D, PEER_HEADS * PEER_QDIM), s),
        'pe_keys': nrm((DEPTH, PEER_HEADS, 2, PEER_NKEYS, PEER_HALF), PEER_HALF ** -0.5),
        'pe_u': nrm((DEPTH, PEER_EXPERTS, D), s),
        'pe_v': nrm((DEPTH, PEER_EXPERTS, D), 0.5),
    }


def reference(x, c, ctx, c_ctx, ada_w, ada_b, norm_mix, norm_ffn, norm_out,
              rw_mu, rw_wrkv, rw_w0, rw_w1, rw_w2, rw_a0, rw_a1, rw_a2, rw_g1, rw_g2,
              rw_kk, rw_ka, rw_rk, rw_lnw, rw_lnb, rw_wo,
              gla_win, gla_wg1, gla_wg2, gla_bg, gla_normw, gla_wo,
              hy_win, hy_bin, hy_scw, hy_scb, hy_f1, hy_fb1, hy_f2, hy_fb2, hy_f3, hy_fb3,
              hy_f4, hy_freq, hy_bias, hy_wo, hy_bo,
              pe_wq, pe_keys, pe_u, pe_v):
    rw = lambda j: (rw_mu[j], rw_wrkv[j], rw_w0[j], rw_w1[j], rw_w2[j], rw_a0[j], rw_a1[j], rw_a2[j],
                    rw_g1[j], rw_g2[j], rw_kk[j], rw_ka[j], rw_rk[j], rw_lnw[j], rw_lnb[j], rw_wo[j])
    gla = lambda j: (gla_win[j], gla_wg1[j], gla_wg2[j], gla_bg[j], gla_normw[j], gla_wo[j])
    hy = lambda j: (hy_win[j], hy_bin[j], hy_scw[j], hy_scb[j], hy_f1[j], hy_fb1[j], hy_f2[j], hy_fb2[j],
                    hy_f3[j], hy_fb3[j], hy_f4[j], hy_freq[j], hy_bias[j], hy_wo[j], hy_bo[j])
    pe = lambda i: (pe_wq[i], pe_keys[i], pe_u[i], pe_v[i])

    xl, xc = x, ctx
    cond = jax.nn.silu(c)
    cond_ctx = jax.nn.silu(c_ctx)
    for i in range(DEPTH):
        kind, j = i % N_MIXERS, i // N_MIXERS
        last = i == DEPTH - 1
        mod = (cond @ ada_w[i] + ada_b[i]).reshape(-1, 1, 6, D_MODEL)
        mod_c = (cond_ctx @ ada_w[i] + ada_b[i]).reshape(1, 1, 6, D_MODEL)
        hl = modulate(xl, norm_mix[i], mod[:, :, 0], mod[:, :, 1])
        if kind == 2:
            yl = hyena_mixer(hl, hy(j))
            yc = None if last else hyena_mixer(modulate(xc, norm_mix[i], mod_c[:, :, 0], mod_c[:, :, 1]), hy(j))
        else:
            hc = modulate(xc, norm_mix[i], mod_c[:, :, 0], mod_c[:, :, 1])
            if kind == 0:
                yc, yl = rwkv_mixer(hc, hl, not last, rw(j))
            else:
                yc, yl = gla_mixer(hc, hl, not last, gla(j))
        xl = xl + mod[:, :, 2] * yl
        xl = xl + mod[:, :, 5] * peer(modulate(xl, norm_ffn[i], mod[:, :, 3], mod[:, :, 4]), pe(i))
        if not last:
            xc = xc + mod_c[:, :, 2] * yc
            xc = xc + mod_c[:, :, 5] * peer(modulate(xc, norm_ffn[i], mod_c[:, :, 3], mod_c[:, :, 4]), pe(i))
    return rmsnorm(xl, norm_out)
```

```python
import math
import functools
import jax, jax.numpy as jnp
from jax import lax
from jax.experimental import pallas as pl
from jax.experimental.pallas import tpu as pltpu

D_MODEL = 1024
BATCH = 8
SEQ = 8192
DEPTH = 4

GRID_W = 64
CTX_LEN = 256
N_MIXERS = 3
NORM_EPS = 1e-6

RW_HEAD = 64
RW_HEADS = D_MODEL // RW_HEAD
RW_GN_EPS = 64e-5

GLA_HEADS = 4
GLA_DK = D_MODEL // 2 // GLA_HEADS
GLA_DV = D_MODEL // GLA_HEADS
GLA_GATE_NORM = 16.0
GLA_CHUNK = 64

HY_ORDER = 2
HY_EMB = 33
HY_BANDS = (HY_EMB - 1) // 2
HY_FAST_DECAY = 0.3
HY_SLOW_DECAY = 1.5
HY_TARGET = 1e-2

PEER_HEADS = 8
PEER_NKEYS = 128
PEER_QDIM = 256
PEER_HALF = PEER_QDIM // 2
PEER_TOPK = 16
PEER_BLOCK = 128


def _final_norm_kernel(x_ref, g_ref, o_ref):
    x = x_ref[...]
    y = x * lax.rsqrt(jnp.mean(x * x, -1, keepdims=True) + NORM_EPS)
    o_ref[...] = y * g_ref[...]


def _final_norm(x, g):
    b_, l, d = x.shape
    rows = b_ * l
    tile = 1024
    out = pl.pallas_call(
        _final_norm_kernel,
        grid=(rows // tile,),
        in_specs=[pl.BlockSpec((tile, d), lambda i: (i, 0)),
                  pl.BlockSpec((1, d), lambda i: (0, 0))],
        out_specs=pl.BlockSpec((tile, d), lambda i: (i, 0)),
        out_shape=jax.ShapeDtypeStruct((rows, d), x.dtype),
        name="final_norm",
    )(x.reshape(rows, d), g.reshape(1, d))
    return out.reshape(b_, l, d)


RW_CHUNK = 64
RW_LANES = 128
_HI = lax.Precision.HIGHEST


def _dot(x, y, precision=None):
    return jnp.dot(x, y, precision=precision, preferred_element_type=jnp.float32)


def _dot_nt(x, y, precision=None):
    return lax.dot_general(x, y, (((1,), (1,)), ((), ())), precision=precision,
                           preferred_element_type=jnp.float32)


def _dot_tn(x, y, precision=None):
    return lax.dot_general(x, y, (((0,), (0,)), ((), ())), precision=precision,
                           preferred_element_type=jnp.float32)


def _unit_triangular_inverse(a):
    c = a.shape[0]
    row = lax.broadcasted_iota(jnp.int32, (c, c), 0)
    col = lax.broadcasted_iota(jnp.int32, (c, c), 1)
    t = jnp.where(row == col, 1.0, 0.0) + a
    p = a
    steps = c.bit_length() - 2
    for _ in range(steps):
        p = _dot(p, p, _HI)
        t = t + _dot(t, p, _HI)
    return t


def _rwkv_chunk_kernel(r_ref, lw_ref, k_ref, v_ref, kk_ref, bb_ref, m0_ref,
                       y_ref, mfin_ref, m_scr, *, reverse):
    f32 = jnp.float32
    c_idx = pl.program_id(2)
    n_chunks = pl.num_programs(2)
    C = RW_CHUNK

    @pl.when(c_idx == 0)
    def _():
        m_scr[...] = m0_ref[0, 0]

    r, lw, k, v, kk, bb = (ref[0] for ref in (r_ref, lw_ref, k_ref, v_ref, kk_ref, bb_ref))
    row = lax.broadcasted_iota(jnp.int32, (C, C), 0)
    col = lax.broadcasted_iota(jnp.int32, (C, C), 1)
    if reverse:
        incl, strict = row <= col, row < col
    else:
        incl, strict = row >= col, row > col
    cum = _dot(jnp.where(incl, 1.0, 0.0).astype(f32), lw, _HI)
    mid = cum[C // 2:C // 2 + 1, :]
    tot = cum[0:1, :] if reverse else cum[C - 1:C, :]
    e_rel = jnp.exp(cum - mid)
    e_inv = jnp.exp(mid - cum)
    a_rel = -kk * jnp.exp(cum - lw - mid)
    b_rel = bb * e_inv
    k_rel = k * e_inv
    r_rel = r * e_rel
    a_abs = -kk * jnp.exp(cum - lw)
    r_abs = r * jnp.exp(cum)
    e_out = jnp.exp(tot - cum)
    k_out = k * e_out
    b_out = bb * e_out

    m = m_scr[...]
    un = _dot(a_abs, m)
    yn = _dot(r_abs, m)
    lane = lax.broadcasted_iota(jnp.int32, (1, RW_LANES), 1)
    head_masks = [jnp.where(lane < RW_HEAD, 1.0, 0.0).astype(f32),
                  jnp.where(lane >= RW_HEAD, 1.0, 0.0).astype(f32)]
    u = jnp.zeros((C, RW_LANES), f32)
    for mh in head_masks:
        a_h = a_rel * mh
        a_ab = jnp.where(strict, _dot_nt(a_h, b_rel), 0.0)
        a_ak = jnp.where(strict, _dot_nt(a_h, k_rel), 0.0)
        rhs = un + _dot(a_ak, v)
        u = u + mh * _dot(_unit_triangular_inverse(a_ab), rhs, _HI)
    y = yn
    for mh in head_masks:
        r_h = r_rel * mh
        r_b = jnp.where(incl, _dot_nt(r_h, b_rel), 0.0)
        r_k = jnp.where(incl, _dot_nt(r_h, k_rel), 0.0)
        y = y + mh * (_dot(r_b, u) + _dot(r_k, v))
    y_ref[0] = y

    ri = lax.broadcasted_iota(jnp.int32, (RW_LANES, RW_LANES), 0)
    ci = lax.broadcasted_iota(jnp.int32, (RW_LANES, RW_LANES), 1)
    decay = jnp.where(ri == ci, jnp.broadcast_to(jnp.exp(tot), (RW_LANES, RW_LANES)), 0.0)
    same_head = (ri < RW_HEAD) == (ci < RW_HEAD)
    m_new = _dot(decay, m, _HI) + jnp.where(same_head, _dot_tn(b_out, u) + _dot_tn(k_out, v), 0.0)
    m_scr[...] = m_new

    @pl.when(c_idx == n_chunks - 1)
    def _():
        mfin_ref[0, 0] = m_new


def rwkv_chunked(r, lw, k, v, kk, bb, s0, reverse, interpret=False):
    b_, l, d = r.shape
    n_pairs = d // RW_LANES
    n_chunks = l // RW_CHUNK
    hn = RW_HEAD
    st = jnp.swapaxes(s0, -1, -2).reshape(b_, n_pairs, 2, hn, hn)
    zero = jnp.zeros_like(st[:, :, 0])
    m0 = jnp.concatenate([jnp.concatenate([st[:, :, 0], zero], -1),
                          jnp.concatenate([zero, st[:, :, 1]], -1)], -2)
    if reverse:
        seq_map = lambda b, p, c: (b, n_chunks - 1 - c, p)
    else:
        seq_map = lambda b, p, c: (b, c, p)
    seq_spec = pl.BlockSpec((1, RW_CHUNK, RW_LANES), seq_map)
    st_spec = pl.BlockSpec((1, 1, RW_LANES, RW_LANES), lambda b, p, c: (b, p, 0, 0))
    y, mfin = pl.pallas_call(
        functools.partial(_rwkv_chunk_kernel, reverse=reverse),
        grid=(b_, n_pairs, n_chunks),
        in_specs=[seq_spec] * 6 + [st_spec],
        out_specs=[seq_spec, st_spec],
        out_shape=[jax.ShapeDtypeStruct((b_, l, d), jnp.float32),
                   jax.ShapeDtypeStruct((b_, n_pairs, RW_LANES, RW_LANES), jnp.float32)],
        scratch_shapes=[pltpu.VMEM((RW_LANES, RW_LANES), jnp.float32)],
        compiler_params=pltpu.CompilerParams(
            dimension_semantics=("parallel", "parallel", "arbitrary")),
        name="rwkv_scan_bwd" if reverse else "rwkv_scan_fwd",
        interpret=interpret,
    )(r, lw, k, v, kk, bb, m0)
    s_fin = jnp.stack([mfin[:, :, :hn, :hn], mfin[:, :, hn:, hn:]], 2)
    return y, jnp.swapaxes(s_fin.reshape(b_, 2 * n_pairs, hn, hn), -1, -2)


PEER_TOK = 8
PEER_SEL = PEER_HEADS * PEER_TOPK
PEER_ROWS = PEER_TOK * PEER_SEL


def _peer_issue(idx_ref, uv_hbm, buf, sem, slot):
    def body(i, carry):
        for half in range(2):
            kq = 2 * i + half
            e = idx_ref[0, 0, kq]
            pltpu.make_async_copy(uv_hbm.at[pl.ds(e, 1), :],
                                  buf.at[slot, pl.ds(kq, 1), :],
                                  sem.at[slot]).start(priority=half)
        return carry
    lax.fori_loop(0, PEER_ROWS // 2, body, 0, unroll=8)


def _peer_expert_kernel(idx_first_ref, idx_next_ref, h_ref, g_ref, uv_hbm, o_ref, buf, sem):
    i = pl.program_id(0)
    n = pl.num_programs(0)
    slot = i % 2

    @pl.when(i == 0)
    def _():
        _peer_issue(idx_first_ref, uv_hbm, buf, sem, 0)

    @pl.when(i + 1 < n)
    def _():
        _peer_issue(idx_next_ref, uv_hbm, buf, sem, 1 - slot)

    pltpu.make_async_copy(uv_hbm.at[pl.ds(0, PEER_ROWS), :], buf.at[slot], sem.at[slot]).wait()

    d = h_ref.shape[1]
    bf16 = jnp.bfloat16
    u = buf[slot, :, 0:d].astype(bf16)
    scores = _dot_nt(h_ref[...].astype(bf16), u)
    tok = lax.broadcasted_iota(jnp.int32, (PEER_TOK, PEER_ROWS), 0)
    owner = lax.broadcasted_iota(jnp.int32, (PEER_TOK, PEER_ROWS), 1) // PEER_SEL
    own = tok == owner
    s_row = jnp.sum(jnp.where(own, scores, 0.0), axis=0, keepdims=True)
    act = 0.5 * s_row * (1.0 + lax.erf(s_row * (2.0 ** -0.5))) * g_ref[0]
    a_mat = jnp.where(own, jnp.broadcast_to(act, (PEER_TOK, PEER_ROWS)), 0.0).astype(bf16)
    vv = buf[slot, :, d:2 * d].astype(bf16)
    o_ref[...] = _dot(a_mat, vv)


def peer_experts(h, e, g, uv, interpret=False):
    n, d = h.shape
    steps = n // PEER_TOK
    idx = e.reshape(steps, 1, PEER_ROWS)
    gate = g.reshape(steps, 1, PEER_ROWS)
    return pl.pallas_call(
        _peer_expert_kernel,
        grid=(steps,),
        in_specs=[
            pl.BlockSpec((1, 1, PEER_ROWS), lambda i: (0, 0, 0), memory_space=pltpu.SMEM),
            pl.BlockSpec((1, 1, PEER_ROWS), lambda i: (jnp.minimum(i + 1, steps - 1), 0, 0),
                         memory_space=pltpu.SMEM),
            pl.BlockSpec((PEER_TOK, d), lambda i: (i, 0)),
            pl.BlockSpec((1, 1, PEER_ROWS), lambda i: (i, 0, 0)),
            pl.BlockSpec(memory_space=pl.ANY),
        ],
        out_specs=pl.BlockSpec((PEER_TOK, d), lambda i: (i, 0)),
        out_shape=jax.ShapeDtypeStruct((n, d), jnp.float32),
        scratch_shapes=[pltpu.VMEM((2, PEER_ROWS, 2 * d), jnp.float32),
                        pltpu.SemaphoreType.DMA((2,))],
        compiler_params=pltpu.CompilerParams(
            dimension_semantics=("arbitrary",),
            vmem_limit_bytes=40 * 1024 * 1024),
        name="peer_experts",
        interpret=interpret,
    )(idx, idx, h, gate, uv)


def rmsnorm(x, g):
    xf = x.astype(jnp.float32)
    y = xf * lax.rsqrt(jnp.mean(xf * xf, -1, keepdims=True) + NORM_EPS)
    return y.astype(x.dtype) * g


def modulate(x, g, shift, scale):
    return rmsnorm(x, g) * (1 + scale) + shift


def shift_grid(h):
    b_, l, d = h.shape
    rows = l // GRID_W
    g = h.reshape(b_, rows, GRID_W, d)
    q = d // 4
    left = jnp.pad(g[:, :, :-1, :q], ((0, 0), (0, 0), (1, 0), (0, 0)))
    right = jnp.pad(g[:, :, 1:, q:2 * q], ((0, 0), (0, 0), (0, 1), (0, 0)))
    up = jnp.pad(g[:, :-1, :, 2 * q:3 * q], ((0, 0), (1, 0), (0, 0), (0, 0)))
    down = jnp.pad(g[:, 1:, :, 3 * q:], ((0, 0), (0, 1), (0, 0), (0, 0)))
    return jnp.concatenate([left, right, up, down], -1).reshape(b_, l, d)


def shift_seq(h):
    d = h.shape[-1] // 2
    prev = jnp.pad(h[:, :-1, :d], ((0, 0), (1, 0), (0, 0)))
    nxt = jnp.pad(h[:, 1:, d:], ((0, 0), (0, 1), (0, 0)))
    return jnp.concatenate([prev, nxt], -1)


def rwkv_mixer(hc, hl, need_ctx, p):
    (mu, w_rkv, w0, w1, w2, a0, a1, a2, g1, g2, k_k, k_a, r_k, ln_w, ln_b, w_o) = p
    f32 = jnp.float32

    def run(h, shifted, s_init):
        b_, l, d = h.shape
        heads = lambda t: t.reshape(b_, l, RW_HEADS, RW_HEAD).astype(f32)
        xx = shifted - h
        rkv = jnp.einsum('jbld,jde->jble', h[None] + xx[None] * mu[:3, None, None, :], w_rkv)
        r, k, v = rkv[0].astype(f32), rkv[1], rkv[2].astype(f32)
        xw, xa, xg = h + xx * mu[3], h + xx * mu[4], h + xx * mu[5]
        kk = heads(k * k_k)
        kk = (kk * lax.rsqrt(jnp.sum(kk * kk, -1, keepdims=True) + 1e-12)).reshape(b_, l, d)
        y = jnp.zeros_like(v)
        k_sum = jnp.zeros_like(v)
        finals = []
        for dr in range(2):
            wlog = -jax.nn.softplus(-(w0[dr] + jnp.tanh(xw @ w1[dr]) @ w2[dr]).astype(f32)) - 0.5
            a = jax.nn.sigmoid((a0[dr] + (xa @ a1[dr]) @ a2[dr]).astype(f32))
            kd = (k * (1 + (a - 1) * k_a)).astype(f32)
            ys, sf = rwkv_chunked(r, -jnp.exp(wlog), kd, v, kk, kk * a, s_init[dr], dr == 1)
            y = y + ys
            k_sum = k_sum + kd
            finals.append(sf)
        return (heads(y), heads(r), heads(v), heads(k_sum), xg), finals

    def post(h, y, r, v, k_sum, xg):
        b_, l, d = h.shape
        mean = jnp.mean(y, -1, keepdims=True)
        var = jnp.mean(jnp.square(y - mean), -1, keepdims=True)
        yn = ((y - mean) * lax.rsqrt(var + RW_GN_EPS)).reshape(b_, l, d) * ln_w + ln_b
        bonus = jnp.sum(r * k_sum * r_k.reshape(RW_HEADS, RW_HEAD).astype(f32), -1, keepdims=True) * v
        g = jax.nn.sigmoid(xg @ g1) @ g2
        return ((yn + bonus.reshape(b_, l, d)).astype(h.dtype) * g) @ w_o

    s_zero = jnp.zeros((hc.shape[0], RW_HEADS, RW_HEAD, RW_HEAD), f32)
    out_c, s_ctx = run(hc, shift_seq(hc), (s_zero, s_zero))
    out_l, _ = run(hl, shift_grid(hl), s_ctx)
    yc = post(hc, *out_c) if need_ctx else None
    return yc, post(hl, *out_l)


def gla_chunked(q, k, v, logd, s0):
    b_, l, h, _ = q.shape
    dv = v.shape[-1]
    c = GLA_CHUNK
    n = l // c
    ch = lambda t: t.reshape(b_, n, c, h, t.shape[-1])
    q, k, v, logd = ch(q), ch(k), ch(v), ch(logd)
    cum = jnp.cumsum(logd, axis=2)
    ref = cum[:, :, c // 2:c // 2 + 1]
    last = cum[:, :, -1:]
    scores = jnp.einsum('bnihd,bnjhd->bnhij', q * jnp.exp(cum - ref), k * jnp.exp(ref - cum))
    scores = jnp.where(jnp.tril(jnp.ones((c, c), bool)), scores, 0.0)
    o_intra = jnp.einsum('bnhij,bnjhv->bnihv', scores, v)
    q_in = q * jnp.exp(cum)
    k_out = k * jnp.exp(last - cum)

    def step(s, inp):
        q_n, k_n, v_n, dec_n = inp
        o = jnp.einsum('bihd,bhdv->bihv', q_n, s)
        s = s * dec_n[..., None] + jnp.einsum('bjhd,bjhv->bhdv', k_n, v_n)
        return s, o

    xs = (jnp.moveaxis(q_in, 1, 0), jnp.moveaxis(k_out, 1, 0), jnp.moveaxis(v, 1, 0),
          jnp.moveaxis(jnp.exp(last[:, :, 0]), 1, 0))
    s_fin, o_inter = lax.scan(step, s0, xs)
    o = o_intra + jnp.moveaxis(o_inter, 0, 1)
    return o.reshape(b_, l, h, dv), s_fin


def gla_mixer(hc, hl, need_ctx, p):
    w_in, wg1, wg2, bg, norm_w, w_o = p
    f32 = jnp.float32
    hk = GLA_HEADS * GLA_DK
    flip = lambda t: jnp.flip(t, 1)

    def run(h, s_init):
        b_, l, d = h.shape
        heads = lambda t, e: t.reshape(b_, l, GLA_HEADS, e).astype(f32)
        q, k, v, og = jnp.split(h @ w_in, [hk, 2 * hk, 2 * hk + d], -1)
        q = heads(q, GLA_DK) * GLA_DK ** -0.5
        k, v = heads(k, GLA_DK), heads(v, GLA_DV)
        logd = [jax.nn.log_sigmoid(heads((h @ wg1[dr]) @ wg2[dr] + bg[dr], GLA_DK)) / GLA_GATE_NORM
                for dr in range(2)]
        o_f, s_f = gla_chunked(q, k, v, logd[0], s_init[0])
        o_b, s_b = gla_chunked(flip(q), flip(k), flip(v), flip(logd[1]), s_init[1])
        return o_f + flip(o_b), og, (s_f, s_b)

    def post(h, o, og):
        b_, l, d = h.shape
        on = o * lax.rsqrt(jnp.mean(o * o, -1, keepdims=True) + NORM_EPS) * norm_w
        return (on.reshape(b_, l, d).astype(h.dtype) * jax.nn.silu(og)) @ w_o

    s_zero = jnp.zeros((hc.shape[0], GLA_HEADS, GLA_DK, GLA_DV), f32)
    oc, ogc, s_ctx = run(hc, (s_zero, s_zero))
    ol, ogl, _ = run(hl, s_ctx)
    yc = post(hc, oc, ogc) if need_ctx else None
    return yc, post(hl, ol, ogl)


def conv3(z, w, b):
    zp = jnp.pad(z, ((0, 0), (1, 1), (0, 0)))
    return zp[:, :-2] * w[0] + zp[:, 1:-1] * w[1] + zp[:, 2:] * w[2] + b


def hyena_filters(l, f1, fb1, f2, fb2, f3, fb3, f4, freq):
    f32 = jnp.float32
    t = jnp.linspace(0.0, 1.0, l, dtype=f32)[:, None]
    w = 2 * math.pi * jnp.arange(l, dtype=f32)[:, None] / l
    f = jnp.linspace(1e-4, HY_BANDS - 1, HY_BANDS, dtype=f32)[None]
    z = jnp.concatenate([t, jnp.cos(f * w), -jnp.sin(f * w)], -1)
    fr = freq.astype(f32)
    hdn = jnp.sin(fr * (z @ f1.astype(f32) + fb1.astype(f32)))
    hdn = jnp.sin(fr * (hdn @ f2.astype(f32) + fb2.astype(f32)))
    hdn = jnp.sin(fr * (hdn @ f3.astype(f32) + fb3.astype(f32)))
    filt = (hdn @ f4.astype(f32)).reshape(l, HY_ORDER, 2, D_MODEL)
    min_decay = math.log(HY_TARGET) / HY_SLOW_DECAY
    max_decay = math.log(HY_TARGET) / HY_FAST_DECAY
    deltas = jnp.linspace(min_decay, max_decay, D_MODEL, dtype=f32)
    filt = filt * jnp.exp(-t * jnp.abs(deltas))[:, None, None, :]
    fwd, bwd = filt[:, :, 0], filt[:, :, 1]
    kc = jnp.concatenate([fwd, jnp.zeros_like(fwd[:1]), jnp.flip(bwd[1:], 0)], 0)
    return kc / jnp.sum(jnp.abs(kc), 0, keepdims=True)


def long_conv(u, kc_o, bias_o):
    l = u.shape[1]
    uf = jnp.fft.rfft(u, n=2 * l, axis=1)
    kf = jnp.fft.rfft(kc_o, n=2 * l, axis=0)
    return jnp.fft.irfft(uf * kf[None], n=2 * l, axis=1)[:, :l] + u * bias_o


def hyena_mixer(h, p):
    (w_in, b_in, sc_w, sc_b, f1, fb1, f2, fb2, f3, fb3, f4, freq, bias, w_o, b_o) = p
    l = h.shape[1]
    z = conv3(h @ w_in + b_in, sc_w, sc_b).astype(jnp.float32)
    v, x1, x2 = jnp.split(z, 3, -1)
    kc = hyena_filters(l, f1, fb1, f2, fb2, f3, fb3, f4, freq)
    bias = bias.astype(jnp.float32)
    y = x1 * long_conv(v, kc[:, 0], bias[0])
    y = x2 * long_conv(y, kc[:, 1], bias[1])
    return y.astype(h.dtype) @ w_o + b_o


def peer(h, p):
    wq, keys, uv = p
    b_, l, d = h.shape
    n = b_ * l
    hf = h.reshape(n, d)
    q = (hf @ wq).reshape(n, PEER_HEADS, 2, PEER_HALF)
    s = jnp.einsum('thpc,hpnc->thpn', q, keys)
    sv, si = lax.top_k(s, PEER_TOPK)
    comb = (sv[:, :, 0, :, None] + sv[:, :, 1, None, :]).reshape(n, PEER_HEADS, PEER_TOPK * PEER_TOPK)
    fv, fi = lax.top_k(comb, PEER_TOPK)
    i1 = jnp.take_along_axis(si[:, :, 0], fi // PEER_TOPK, axis=-1)
    i2 = jnp.take_along_axis(si[:, :, 1], fi % PEER_TOPK, axis=-1)
    e = (i1 * PEER_NKEYS + i2).astype(jnp.int32).reshape(n, PEER_SEL)
    g = jax.nn.softmax(fv.astype(jnp.float32), -1).reshape(n, PEER_SEL)
    return peer_experts(hf, e, g, uv).reshape(b_, l, d)


def kernel(x, c, ctx, c_ctx, ada_w, ada_b, norm_mix, norm_ffn, norm_out,
           rw_mu, rw_wrkv, rw_w0, rw_w1, rw_w2, rw_a0, rw_a1, rw_a2, rw_g1, rw_g2,
           rw_kk, rw_ka, rw_rk, rw_lnw, rw_lnb, rw_wo,
           gla_win, gla_wg1, gla_wg2, gla_bg, gla_normw, gla_wo,
           hy_win, hy_bin, hy_scw, hy_scb, hy_f1, hy_fb1, hy_f2, hy_fb2, hy_f3, hy_fb3,
           hy_f4, hy_freq, hy_bias, hy_wo, hy_bo,
           pe_wq, pe_keys, pe_u, pe_v):
    rw = lambda j: (rw_mu[j], rw_wrkv[j], rw_w0[j], rw_w1[j], rw_w2[j], rw_a0[j], rw_a1[j], rw_a2[j],
                    rw_g1[j], rw_g2[j], rw_kk[j], rw_ka[j], rw_rk[j], rw_lnw[j], rw_lnb[j], rw_wo[j])
    gla = lambda j: (gla_win[j], gla_wg1[j], gla_wg2[j], gla_bg[j], gla_normw[j], gla_wo[j])
    hy = lambda j: (hy_win[j], hy_bin[j], hy_scw[j], hy_scb[j], hy_f1[j], hy_fb1[j], hy_f2[j], hy_fb2[j],
                    hy_f3[j], hy_fb3[j], hy_f4[j], hy_freq[j], hy_bias[j], hy_wo[j], hy_bo[j])
    pe_uv = [jnp.concatenate([pe_u[i], pe_v[i]], axis=1) for i in range(DEPTH)]
    pe = lambda i: (pe_wq[i], pe_keys[i], pe_uv[i])

    xl, xc = x, ctx
    cond = jax.nn.silu(c)
    cond_ctx = jax.nn.silu(c_ctx)
    for i in range(DEPTH):
        kind, j = i % N_MIXERS, i // N_MIXERS
        last = i == DEPTH - 1
        mod = (cond @ ada_w[i] + ada_b[i]).reshape(-1, 1, 6, D_MODEL)
        mod_c = (cond_ctx @ ada_w[i] + ada_b[i]).reshape(1, 1, 6, D_MODEL)
        hl = modulate(xl, norm_mix[i], mod[:, :, 0], mod[:, :, 1])
        if kind == 2:
            yl = hyena_mixer(hl, hy(j))
            yc = None if last else hyena_mixer(modulate(xc, norm_mix[i], mod_c[:, :, 0], mod_c[:, :, 1]), hy(j))
        else:
            hc = modulate(xc, norm_mix[i], mod_c[:, :, 0], mod_c[:, :, 1])
            if kind == 0:
                yc, yl = rwkv_mixer(hc, hl, not last, rw(j))
            else:
                yc, yl = gla_mixer(hc, hl, not last, gla(j))
        xl = xl + mod[:, :, 2] * yl
        xl = xl + mod[:, :, 5] * peer(modulate(xl, norm_ffn[i], mod[:, :, 3], mod[:, :, 4]), pe(i))
        if not last:
            xc = xc + mod_c[:, :, 2] * yc
            xc = xc + mod_c[:, :, 5] * peer(modulate(xc, norm_ffn[i], mod_c[:, :, 3], mod_c[:, :, 4]), pe(i))
    return _final_norm(xl, norm_out)
```

```python
import math
import functools
import jax, jax.numpy as jnp
from jax import lax
from jax.experimental import pallas as pl
from jax.experimental.pallas import tpu as pltpu

D_MODEL = 1024
BATCH = 8
SEQ = 8192
DEPTH = 4

GRID_W = 64
CTX_LEN = 256
N_MIXERS = 3
NORM_EPS = 1e-6

RW_HEAD = 64
RW_HEADS = D_MODEL // RW_HEAD
RW_GN_EPS = 64e-5

GLA_HEADS = 4
GLA_DK = D_MODEL // 2 // GLA_HEADS
GLA_DV = D_MODEL // GLA_HEADS
GLA_GATE_NORM = 16.0
GLA_CHUNK = 64

HY_ORDER = 2
HY_EMB = 33
HY_BANDS = (HY_EMB - 1) // 2
HY_FAST_DECAY = 0.3
HY_SLOW_DECAY = 1.5
HY_TARGET = 1e-2

PEER_HEADS = 8
PEER_NKEYS = 128
PEER_QDIM = 256
PEER_HALF = PEER_QDIM // 2
PEER_TOPK = 16
PEER_BLOCK = 128


def _final_norm_kernel(x_ref, g_ref, o_ref):
    x = x_ref[...]
    y = x * lax.rsqrt(jnp.mean(x * x, -1, keepdims=True) + NORM_EPS)
    o_ref[...] = y * g_ref[...]


def _final_norm(x, g):
    b_, l, d = x.shape
    rows = b_ * l
    tile = 1024
    out = pl.pallas_call(
        _final_norm_kernel,
        grid=(rows // tile,),
        in_specs=[pl.BlockSpec((tile, d), lambda i: (i, 0)),
                  pl.BlockSpec((1, d), lambda i: (0, 0))],
        out_specs=pl.BlockSpec((tile, d), lambda i: (i, 0)),
        out_shape=jax.ShapeDtypeStruct((rows, d), x.dtype),
        name="final_norm",
    )(x.reshape(rows, d), g.reshape(1, d))
    return out.reshape(b_, l, d)


RW_CHUNK = 64
RW_LANES = 128
RW_PAIRS_PER_STEP = 4
_HI = lax.Precision.HIGHEST


def _dot(x, y, precision=None):
    return jnp.dot(x, y, precision=precision, preferred_element_type=jnp.float32)


def _dot_nt(x, y, precision=None):
    return lax.dot_general(x, y, (((1,), (1,)), ((), ())), precision=precision,
                           preferred_element_type=jnp.float32)


def _dot_tn(x, y, precision=None):
    return lax.dot_general(x, y, (((0,), (0,)), ((), ())), precision=precision,
                           preferred_element_type=jnp.float32)


def _unit_triangular_inverses(mats):
    c = mats[0].shape[0]
    row = lax.broadcasted_iota(jnp.int32, (c, c), 0)
    col = lax.broadcasted_iota(jnp.int32, (c, c), 1)
    eye = jnp.where(row == col, 1.0, 0.0)
    ts = [eye + a for a in mats]
    ps = list(mats)
    steps = c.bit_length() - 2
    for _ in range(steps):
        ps = [_dot(p, p, _HI) for p in ps]
        ts = [t + _dot(t, p, _HI) for t, p in zip(ts, ps)]
    return ts


def _rwkv_chunk_pairs(seqs, ms, reverse):
    f32 = jnp.float32
    C = RW_CHUNK
    row = lax.broadcasted_iota(jnp.int32, (C, C), 0)
    col = lax.broadcasted_iota(jnp.int32, (C, C), 1)
    if reverse:
        incl, strict = row <= col, row < col
    else:
        incl, strict = row >= col, row > col
    tri = jnp.where(incl, 1.0, 0.0).astype(f32)
    lane = lax.broadcasted_iota(jnp.int32, (1, RW_LANES), 1)
    head_masks = [jnp.where(lane < RW_HEAD, 1.0, 0.0).astype(f32),
                  jnp.where(lane >= RW_HEAD, 1.0, 0.0).astype(f32)]
    ri = lax.broadcasted_iota(jnp.int32, (RW_LANES, RW_LANES), 0)
    ci = lax.broadcasted_iota(jnp.int32, (RW_LANES, RW_LANES), 1)
    same_head = (ri < RW_HEAD) == (ci < RW_HEAD)
    n = len(seqs)

    cums = [_dot(tri, s[1], _HI) for s in seqs]
    pre = []
    for (r, lw, k, v, kk, bb), cum in zip(seqs, cums):
        mid = cum[C // 2:C // 2 + 1, :]
        tot = cum[0:1, :] if reverse else cum[C - 1:C, :]
        e_inv = jnp.exp(mid - cum)
        e_out = jnp.exp(tot - cum)
        pre.append(dict(
            tot=tot, v=v,
            a_rel=-kk * jnp.exp(cum - lw - mid), b_rel=bb * e_inv, k_rel=k * e_inv,
            r_rel=r * jnp.exp(cum - mid), a_abs=-kk * jnp.exp(cum - lw), r_abs=r * jnp.exp(cum),
            k_out=k * e_out, b_out=bb * e_out))
    uns = [_dot(p["a_abs"], m) for p, m in zip(pre, ms)]
    yns = [_dot(p["r_abs"], m) for p, m in zip(pre, ms)]
    a_hs = [[p["a_rel"] * mh for mh in head_masks] for p in pre]
    a_abs_ = [[jnp.where(strict, _dot_nt(a_h, p["b_rel"]), 0.0) for a_h in a_hs[i]] for i, p in enumerate(pre)]
    a_aks = [[jnp.where(strict, _dot_nt(a_h, p["k_rel"]), 0.0) for a_h in a_hs[i]] for i, p in enumerate(pre)]
    rhss = [[uns[i] + _dot(a_ak, p["v"]) for a_ak in a_aks[i]] for i, p in enumerate(pre)]
    invs = _unit_triangular_inverses([a for pair in a_abs_ for a in pair])
    us = []
    for i in range(n):
        parts = [_dot(invs[2 * i + h], rhss[i][h], _HI) for h in range(2)]
        us.append(head_masks[0] * parts[0] + head_masks[1] * parts[1])
    r_hs = [[p["r_rel"] * mh for mh in head_masks] for p in pre]
    r_bs = [[jnp.where(incl, _dot_nt(r_h, p["b_rel"]), 0.0) for r_h in r_hs[i]] for i, p in enumerate(pre)]
    r_ks = [[jnp.where(incl, _dot_nt(r_h, p["k_rel"]), 0.0) for r_h in r_hs[i]] for i, p in enumerate(pre)]
    out = []
    for i, p in enumerate(pre):
        y = yns[i]
        for h, mh in enumerate(head_masks):
            y = y + mh * (_dot(r_bs[i][h], us[i]) + _dot(r_ks[i][h], p["v"]))
        decay = jnp.where(ri == ci, jnp.broadcast_to(jnp.exp(p["tot"]), (RW_LANES, RW_LANES)), 0.0)
        m_new = _dot(decay, ms[i], _HI) + jnp.where(
            same_head, _dot_tn(p["b_out"], us[i]) + _dot_tn(p["k_out"], p["v"]), 0.0)
        out.append((y, m_new))
    return out


def _rwkv_chunk_kernel(r_ref, lw_ref, k_ref, v_ref, kk_ref, bb_ref, m0_ref,
                       y_ref, mfin_ref, m_scr, *, reverse):
    c_idx = pl.program_id(2)
    n_chunks = pl.num_programs(2)

    @pl.when(c_idx == 0)
    def _():
        m_scr[...] = m0_ref[0]

    lanes = [slice(p * RW_LANES, (p + 1) * RW_LANES) for p in range(RW_PAIRS_PER_STEP)]
    seqs = [tuple(ref[0, :, sl] for ref in (r_ref, lw_ref, k_ref, v_ref, kk_ref, bb_ref)) for sl in lanes]
    results = _rwkv_chunk_pairs(seqs, [m_scr[p] for p in range(RW_PAIRS_PER_STEP)], reverse)
    for p, (sl, (y, m_new)) in enumerate(zip(lanes, results)):
        y_ref[0, :, sl] = y
        m_scr[p] = m_new

    @pl.when(c_idx == n_chunks - 1)
    def _():
        mfin_ref[0] = m_scr[...]


def rwkv_chunked(r, lw, k, v, kk, bb, s0, reverse, interpret=False):
    b_, l, d = r.shape
    n_pairs = d // RW_LANES
    n_chunks = l // RW_CHUNK
    hn = RW_HEAD
    st = jnp.swapaxes(s0, -1, -2).reshape(b_, n_pairs, 2, hn, hn)
    zero = jnp.zeros_like(st[:, :, 0])
    m0 = jnp.concatenate([jnp.concatenate([st[:, :, 0], zero], -1),
                          jnp.concatenate([zero, st[:, :, 1]], -1)], -2)
    if reverse:
        seq_map = lambda b, p, c: (b, n_chunks - 1 - c, p)
    else:
        seq_map = lambda b, p, c: (b, c, p)
    pps = RW_PAIRS_PER_STEP
    seq_spec = pl.BlockSpec((1, RW_CHUNK, pps * RW_LANES), seq_map)
    st_spec = pl.BlockSpec((1, pps, RW_LANES, RW_LANES), lambda b, p, c: (b, p, 0, 0))
    y, mfin = pl.pallas_call(
        functools.partial(_rwkv_chunk_kernel, reverse=reverse),
        grid=(b_, n_pairs // pps, n_chunks),
        in_specs=[seq_spec] * 6 + [st_spec],
        out_specs=[seq_spec, st_spec],
        out_shape=[jax.ShapeDtypeStruct((b_, l, d), jnp.float32),
                   jax.ShapeDtypeStruct((b_, n_pairs, RW_LANES, RW_LANES), jnp.float32)],
        scratch_shapes=[pltpu.VMEM((pps, RW_LANES, RW_LANES), jnp.float32)],
        compiler_params=pltpu.CompilerParams(
            dimension_semantics=("parallel", "parallel", "arbitrary")),
        name="rwkv_scan_bwd" if reverse else "rwkv_scan_fwd",
        interpret=interpret,
    )(r, lw, k, v, kk, bb, m0)
    s_fin = jnp.stack([mfin[:, :, :hn, :hn], mfin[:, :, hn:, hn:]], 2)
    return y, jnp.swapaxes(s_fin.reshape(b_, 2 * n_pairs, hn, hn), -1, -2)


PEER_TOK = 8
PEER_SEL = PEER_HEADS * PEER_TOPK
PEER_ROWS = PEER_TOK * PEER_SEL
SUBLANES = 8
LANES = 128
PEER_GROUPS = PEER_ROWS // SUBLANES
PEER_DTILES = D_MODEL // LANES


def _peer_issue(idx_ref, uv_hbm, buf, sem, slot):
    def body(grp, carry):
        for sub in range(SUBLANES):
            e = idx_ref[0, 0, grp * SUBLANES + sub]
            pltpu.make_async_copy(uv_hbm.at[e], buf.at[slot, grp, :, sub, :],
                                  sem.at[slot]).start(priority=sub % 2)
        return carry
    lax.fori_loop(0, PEER_GROUPS, body, 0, unroll=2)


def _peer_expert_kernel(idx_first_ref, idx_next_ref, h_ref, g_ref, uv_hbm, o_ref, buf, sem):
    i = pl.program_id(0)
    n = pl.num_programs(0)
    slot = i % 2

    @pl.when(i == 0)
    def _():
        _peer_issue(idx_first_ref, uv_hbm, buf, sem, 0)

    @pl.when(i + 1 < n)
    def _():
        _peer_issue(idx_next_ref, uv_hbm, buf, sem, 1 - slot)

    pltpu.make_async_copy(buf.at[1 - slot], buf.at[slot], sem.at[slot]).wait()

    bf16 = jnp.bfloat16
    h = h_ref[...].astype(bf16)
    scores = jnp.zeros((PEER_TOK, PEER_ROWS), jnp.float32)
    for c in range(PEER_DTILES):
        u_c = buf[slot, :, c, :, :].reshape(PEER_ROWS, LANES).astype(bf16)
        scores = scores + _dot_nt(h[:, c * LANES:(c + 1) * LANES], u_c)
    tok = lax.broadcasted_iota(jnp.int32, (PEER_TOK, PEER_ROWS), 0)
    owner = lax.broadcasted_iota(jnp.int32, (PEER_TOK, PEER_ROWS), 1) // PEER_SEL
    own = tok == owner
    s_row = jnp.sum(jnp.where(own, scores, 0.0), axis=0, keepdims=True)
    act = 0.5 * s_row * (1.0 + lax.erf(s_row * (2.0 ** -0.5))) * g_ref[0]
    a_mat = jnp.where(own, jnp.broadcast_to(act, (PEER_TOK, PEER_ROWS)), 0.0).astype(bf16)
    for c in range(PEER_DTILES):
        v_c = buf[slot, :, PEER_DTILES + c, :, :].reshape(PEER_ROWS, LANES).astype(bf16)
        o_ref[:, c * LANES:(c + 1) * LANES] = _dot(a_mat, v_c)


def peer_experts(h, e, g, uv, interpret=False):
    n, d = h.shape
    steps = n // PEER_TOK
    idx = e.reshape(steps, 1, PEER_ROWS)
    gate = g.reshape(steps, 1, PEER_ROWS)
    return pl.pallas_call(
        _peer_expert_kernel,
        grid=(steps,),
        in_specs=[
            pl.BlockSpec((1, 1, PEER_ROWS), lambda i: (0, 0, 0), memory_space=pltpu.SMEM),
            pl.BlockSpec((1, 1, PEER_ROWS), lambda i: (jnp.minimum(i + 1, steps - 1), 0, 0),
                         memory_space=pltpu.SMEM),
            pl.BlockSpec((PEER_TOK, d), lambda i: (i, 0)),
            pl.BlockSpec((1, 1, PEER_ROWS), lambda i: (i, 0, 0)),
            pl.BlockSpec(memory_space=pl.ANY),
        ],
        out_specs=pl.BlockSpec((PEER_TOK, d), lambda i: (i, 0)),
        out_shape=jax.ShapeDtypeStruct((n, d), jnp.float32),
        scratch_shapes=[pltpu.VMEM((2, PEER_GROUPS, 2 * PEER_DTILES, SUBLANES, LANES), jnp.float32),
                        pltpu.SemaphoreType.DMA((2,))],
        compiler_params=pltpu.CompilerParams(
            dimension_semantics=("arbitrary",),
            vmem_limit_bytes=40 * 1024 * 1024),
        name="peer_experts",
        interpret=interpret,
    )(idx, idx, h, gate, uv)


ROUTE_TOK = 128


def _top_rounds(s, rounds):
    n_rows, t = s.shape
    row = lax.broadcasted_iota(jnp.int32, (n_rows, t), 0)
    out_row = lax.broadcasted_iota(jnp.int32, (rounds, t), 0)
    vals = jnp.zeros((rounds, t), jnp.float32)
    idxs = jnp.zeros((rounds, t), jnp.int32)
    for r in range(rounds):
        m = jnp.max(s, axis=0, keepdims=True)
        i = jnp.min(jnp.where(s == m, row, n_rows), axis=0, keepdims=True)
        vals = jnp.where(out_row == r, m, vals)
        idxs = jnp.where(out_row == r, i, idxs)
        s = jnp.where(row == i, -jnp.inf, s)
    return vals, idxs


def _select_rows(table, sel):
    k, t = table.shape
    out = jnp.zeros(sel.shape, table.dtype)
    for j in range(k):
        out = jnp.where(sel == j, table[j:j + 1, :], out)
    return out


def _peer_route_kernel(h_ref, wq_ref, keys_ref, e_ref, g_ref):
    q = _dot(h_ref[...], wq_ref[...])
    e_rows, g_rows = [], []
    for hd in range(PEER_HEADS):
        sv, si = [], []
        for p in range(2):
            c0 = (hd * 2 + p) * PEER_HALF
            s_t = _dot_nt(keys_ref[hd, p], q[:, c0:c0 + PEER_HALF])
            v_, i_ = _top_rounds(s_t, PEER_TOPK)
            sv.append(v_)
            si.append(i_)
        comb = jnp.concatenate([sv[0][a:a + 1, :] + sv[1] for a in range(PEER_TOPK)], axis=0)
        fv, fi = _top_rounds(comb, PEER_TOPK)
        i1 = _select_rows(si[0], fi // PEER_TOPK)
        i2 = _select_rows(si[1], fi % PEER_TOPK)
        e_rows.append(i1 * PEER_NKEYS + i2)
        ex = jnp.exp(fv - jnp.max(fv, axis=0, keepdims=True))
        g_rows.append(ex / jnp.sum(ex, axis=0, keepdims=True))
    e_ref[...] = jnp.concatenate(e_rows, axis=0).T
    g_ref[...] = jnp.concatenate(g_rows, axis=0).T


def peer_route(h, wq, keys, interpret=False):
    n, d = h.shape
    return pl.pallas_call(
        _peer_route_kernel,
        grid=(n // ROUTE_TOK,),
        in_specs=[pl.BlockSpec((ROUTE_TOK, d), lambda i: (i, 0)),
                  pl.BlockSpec(wq.shape, lambda i: (0, 0)),
                  pl.BlockSpec(keys.shape, lambda i: (0, 0, 0, 0))],
        out_specs=[pl.BlockSpec((ROUTE_TOK, PEER_SEL), lambda i: (i, 0)),
                   pl.BlockSpec((ROUTE_TOK, PEER_SEL), lambda i: (i, 0))],
        out_shape=[jax.ShapeDtypeStruct((n, PEER_SEL), jnp.int32),
                   jax.ShapeDtypeStruct((n, PEER_SEL), jnp.float32)],
        compiler_params=pltpu.CompilerParams(
            dimension_semantics=("parallel",), vmem_limit_bytes=40 * 1024 * 1024),
        name="peer_route",
        interpret=interpret,
    )(h, wq, keys)


def rmsnorm(x, g):
    xf = x.astype(jnp.float32)
    y = xf * lax.rsqrt(jnp.mean(xf * xf, -1, keepdims=True) + NORM_EPS)
    return y.astype(x.dtype) * g


def modulate(x, g, shift, scale):
    return rmsnorm(x, g) * (1 + scale) + shift


def shift_grid(h):
    b_, l, d = h.shape
    rows = l // GRID_W
    g = h.reshape(b_, rows, GRID_W, d)
    q = d // 4
    left = jnp.pad(g[:, :, :-1, :q], ((0, 0), (0, 0), (1, 0), (0, 0)))
    right = jnp.pad(g[:, :, 1:, q:2 * q], ((0, 0), (0, 0), (0, 1), (0, 0)))
    up = jnp.pad(g[:, :-1, :, 2 * q:3 * q], ((0, 0), (1, 0), (0, 0), (0, 0)))
    down = jnp.pad(g[:, 1:, :, 3 * q:], ((0, 0), (0, 1), (0, 0), (0, 0)))
    return jnp.concatenate([left, right, up, down], -1).reshape(b_, l, d)


def shift_seq(h):
    d = h.shape[-1] // 2
    prev = jnp.pad(h[:, :-1, :d], ((0, 0), (1, 0), (0, 0)))
    nxt = jnp.pad(h[:, 1:, d:], ((0, 0), (0, 1), (0, 0)))
    return jnp.concatenate([prev, nxt], -1)


def rwkv_mixer(hc, hl, need_ctx, p):
    (mu, w_rkv, w0, w1, w2, a0, a1, a2, g1, g2, k_k, k_a, r_k, ln_w, ln_b, w_o) = p
    f32 = jnp.float32

    def run(h, shifted, s_init):
        b_, l, d = h.shape
        heads = lambda t: t.reshape(b_, l, RW_HEADS, RW_HEAD).astype(f32)
        xx = shifted - h
        rkv = jnp.einsum('jbld,jde->jble', h[None] + xx[None] * mu[:3, None, None, :], w_rkv)
        r, k, v = rkv[0].astype(f32), rkv[1], rkv[2].astype(f32)
        xw, xa, xg = h + xx * mu[3], h + xx * mu[4], h + xx * mu[5]
        kk = heads(k * k_k)
        kk = (kk * lax.rsqrt(jnp.sum(kk * kk, -1, keepdims=True) + 1e-12)).reshape(b_, l, d)
        y = jnp.zeros_like(v)
        k_sum = jnp.zeros_like(v)
        finals = []
        for dr in range(2):
            wlog = -jax.nn.softplus(-(w0[dr] + jnp.tanh(xw @ w1[dr]) @ w2[dr]).astype(f32)) - 0.5
            a = jax.nn.sigmoid((a0[dr] + (xa @ a1[dr]) @ a2[dr]).astype(f32))
            kd = (k * (1 + (a - 1) * k_a)).astype(f32)
            ys, sf = rwkv_chunked(r, -jnp.exp(wlog), kd, v, kk, kk * a, s_init[dr], dr == 1)
            y = y + ys
            k_sum = k_sum + kd
            finals.append(sf)
        return (heads(y), heads(r), heads(v), heads(k_sum), xg), finals

    def post(h, y, r, v, k_sum, xg):
        b_, l, d = h.shape
        mean = jnp.mean(y, -1, keepdims=True)
        var = jnp.mean(jnp.square(y - mean), -1, keepdims=True)
        yn = ((y - mean) * lax.rsqrt(var + RW_GN_EPS)).reshape(b_, l, d) * ln_w + ln_b
        bonus = jnp.sum(r * k_sum * r_k.reshape(RW_HEADS, RW_HEAD).astype(f32), -1, keepdims=True) * v
        g = jax.nn.sigmoid(xg @ g1) @ g2
        return ((yn + bonus.reshape(b_, l, d)).astype(h.dtype) * g) @ w_o

    s_zero = jnp.zeros((hc.shape[0], RW_HEADS, RW_HEAD, RW_HEAD), f32)
    out_c, s_ctx = run(hc, shift_seq(hc), (s_zero, s_zero))
    out_l, _ = run(hl, shift_grid(hl), s_ctx)
    yc = post(hc, *out_c) if need_ctx else None
    return yc, post(hl, *out_l)


def gla_chunked(q, k, v, logd, s0):
    b_, l, h, _ = q.shape
    dv = v.shape[-1]
    c = GLA_CHUNK
    n = l // c
    ch = lambda t: t.reshape(b_, n, c, h, t.shape[-1])
    q, k, v, logd = ch(q), ch(k), ch(v), ch(logd)
    cum = jnp.cumsum(logd, axis=2)
    ref = cum[:, :, c // 2:c // 2 + 1]
    last = cum[:, :, -1:]
    scores = jnp.einsum('bnihd,bnjhd->bnhij', q * jnp.exp(cum - ref), k * jnp.exp(ref - cum))
    scores = jnp.where(jnp.tril(jnp.ones((c, c), bool)), scores, 0.0)
    o_intra = jnp.einsum('bnhij,bnjhv->bnihv', scores, v)
    q_in = q * jnp.exp(cum)
    k_out = k * jnp.exp(last - cum)

    def step(s, inp):
        q_n, k_n, v_n, dec_n = inp
        o = jnp.einsum('bihd,bhdv->bihv', q_n, s)
        s = s * dec_n[..., None] + jnp.einsum('bjhd,bjhv->bhdv', k_n, v_n)
        return s, o

    xs = (jnp.moveaxis(q_in, 1, 0), jnp.moveaxis(k_out, 1, 0), jnp.moveaxis(v, 1, 0),
          jnp.moveaxis(jnp.exp(last[:, :, 0]), 1, 0))
    s_fin, o_inter = lax.scan(step, s0, xs)
    o = o_intra + jnp.moveaxis(o_inter, 0, 1)
    return o.reshape(b_, l, h, dv), s_fin


def gla_mixer(hc, hl, need_ctx, p):
    w_in, wg1, wg2, bg, norm_w, w_o = p
    f32 = jnp.float32
    hk = GLA_HEADS * GLA_DK
    flip = lambda t: jnp.flip(t, 1)

    def run(h, s_init):
        b_, l, d = h.shape
        heads = lambda t, e: t.reshape(b_, l, GLA_HEADS, e).astype(f32)
        q, k, v, og = jnp.split(h @ w_in, [hk, 2 * hk, 2 * hk + d], -1)
        q = heads(q, GLA_DK) * GLA_DK ** -0.5
        k, v = heads(k, GLA_DK), heads(v, GLA_DV)
        logd = [jax.nn.log_sigmoid(heads((h @ wg1[dr]) @ wg2[dr] + bg[dr], GLA_DK)) / GLA_GATE_NORM
                for dr in range(2)]
        o_f, s_f = gla_chunked(q, k, v, logd[0], s_init[0])
        o_b, s_b = gla_chunked(flip(q), flip(k), flip(v), flip(logd[1]), s_init[1])
        return o_f + flip(o_b), og, (s_f, s_b)

    def post(h, o, og):
        b_, l, d = h.shape
        on = o * lax.rsqrt(jnp.mean(o * o, -1, keepdims=True) + NORM_EPS) * norm_w
        return (on.reshape(b_, l, d).astype(h.dtype) * jax.nn.silu(og)) @ w_o

    s_zero = jnp.zeros((hc.shape[0], GLA_HEADS, GLA_DK, GLA_DV), f32)
    oc, ogc, s_ctx = run(hc, (s_zero, s_zero))
    ol, ogl, _ = run(hl, s_ctx)
    yc = post(hc, oc, ogc) if need_ctx else None
    return yc, post(hl, ol, ogl)


def conv3(z, w, b):
    zp = jnp.pad(z, ((0, 0), (1, 1), (0, 0)))
    return zp[:, :-2] * w[0] + zp[:, 1:-1] * w[1] + zp[:, 2:] * w[2] + b


def hyena_filters(l, f1, fb1, f2, fb2, f3, fb3, f4, freq):
    f32 = jnp.float32
    t = jnp.linspace(0.0, 1.0, l, dtype=f32)[:, None]
    w = 2 * math.pi * jnp.arange(l, dtype=f32)[:, None] / l
    f = jnp.linspace(1e-4, HY_BANDS - 1, HY_BANDS, dtype=f32)[None]
    z = jnp.concatenate([t, jnp.cos(f * w), -jnp.sin(f * w)], -1)
    fr = freq.astype(f32)
    hdn = jnp.sin(fr * (z @ f1.astype(f32) + fb1.astype(f32)))
    hdn = jnp.sin(fr * (hdn @ f2.astype(f32) + fb2.astype(f32)))
    hdn = jnp.sin(fr * (hdn @ f3.astype(f32) + fb3.astype(f32)))
    filt = (hdn @ f4.astype(f32)).reshape(l, HY_ORDER, 2, D_MODEL)
    min_decay = math.log(HY_TARGET) / HY_SLOW_DECAY
    max_decay = math.log(HY_TARGET) / HY_FAST_DECAY
    deltas = jnp.linspace(min_decay, max_decay, D_MODEL, dtype=f32)
    filt = filt * jnp.exp(-t * jnp.abs(deltas))[:, None, None, :]
    fwd, bwd = filt[:, :, 0], filt[:, :, 1]
    kc = jnp.concatenate([fwd, jnp.zeros_like(fwd[:1]), jnp.flip(bwd[1:], 0)], 0)
    return kc / jnp.sum(jnp.abs(kc), 0, keepdims=True)


def long_conv(u, kc_o, bias_o):
    l = u.shape[1]
    uf = jnp.fft.rfft(u, n=2 * l, axis=1)
    kf = jnp.fft.rfft(kc_o, n=2 * l, axis=0)
    return jnp.fft.irfft(uf * kf[None], n=2 * l, axis=1)[:, :l] + u * bias_o


def hyena_mixer(h, p):
    (w_in, b_in, sc_w, sc_b, f1, fb1, f2, fb2, f3, fb3, f4, freq, bias, w_o, b_o) = p
    l = h.shape[1]
    z = conv3(h @ w_in + b_in, sc_w, sc_b).astype(jnp.float32)
    v, x1, x2 = jnp.split(z, 3, -1)
    kc = hyena_filters(l, f1, fb1, f2, fb2, f3, fb3, f4, freq)
    bias = bias.astype(jnp.float32)
    y = x1 * long_conv(v, kc[:, 0], bias[0])
    y = x2 * long_conv(y, kc[:, 1], bias[1])
    return y.astype(h.dtype) @ w_o + b_o


def peer(h, p):
    wq, keys, uv = p
    b_, l, d = h.shape
    n = b_ * l
    hf = h.reshape(n, d)
    e, g = peer_route(hf, wq, keys)
    return peer_experts(hf, e, g, uv).reshape(b_, l, d)


def kernel(x, c, ctx, c_ctx, ada_w, ada_b, norm_mix, norm_ffn, norm_out,
           rw_mu, rw_wrkv, rw_w0, rw_w1, rw_w2, rw_a0, rw_a1, rw_a2, rw_g1, rw_g2,
           rw_kk, rw_ka, rw_rk, rw_lnw, rw_lnb, rw_wo,
           gla_win, gla_wg1, gla_wg2, gla_bg, gla_normw, gla_wo,
           hy_win, hy_bin, hy_scw, hy_scb, hy_f1, hy_fb1, hy_f2, hy_fb2, hy_f3, hy_fb3,
           hy_f4, hy_freq, hy_bias, hy_wo, hy_bo,
           pe_wq, pe_keys, pe_u, pe_v):
    rw = lambda j: (rw_mu[j], rw_wrkv[j], rw_w0[j], rw_w1[j], rw_w2[j], rw_a0[j], rw_a1[j], rw_a2[j],
                    rw_g1[j], rw_g2[j], rw_kk[j], rw_ka[j], rw_rk[j], rw_lnw[j], rw_lnb[j], rw_wo[j])
    gla = lambda j: (gla_win[j], gla_wg1[j], gla_wg2[j], gla_bg[j], gla_normw[j], gla_wo[j])
    hy = lambda j: (hy_win[j], hy_bin[j], hy_scw[j], hy_scb[j], hy_f1[j], hy_fb1[j], hy_f2[j], hy_fb2[j],
                    hy_f3[j], hy_fb3[j], hy_f4[j], hy_freq[j], hy_bias[j], hy_wo[j], hy_bo[j])
    pe_uv = [jnp.concatenate([pe_u[i], pe_v[i]], axis=1).reshape(-1, 2 * PEER_DTILES, LANES)
             for i in range(DEPTH)]
    pe = lambda i: (pe_wq[i], pe_keys[i], pe_uv[i])

    xl, xc = x, ctx
    cond = jax.nn.silu(c)
    cond_ctx = jax.nn.silu(c_ctx)
    for i in range(DEPTH):
        kind, j = i % N_MIXERS, i // N_MIXERS
        last = i == DEPTH - 1
        mod = (cond @ ada_w[i] + ada_b[i]).reshape(-1, 1, 6, D_MODEL)
        mod_c = (cond_ctx @ ada_w[i] + ada_b[i]).reshape(1, 1, 6, D_MODEL)
        hl = modulate(xl, norm_mix[i], mod[:, :, 0], mod[:, :, 1])
        if kind == 2:
            yl = hyena_mixer(hl, hy(j))
            yc = None if last else hyena_mixer(modulate(xc, norm_mix[i], mod_c[:, :, 0], mod_c[:, :, 1]), hy(j))
        else:
            hc = modulate(xc, norm_mix[i], mod_c[:, :, 0], mod_c[:, :, 1])
            if kind == 0:
                yc, yl = rwkv_mixer(hc, hl, not last, rw(j))
            else:
                yc, yl = gla_mixer(hc, hl, not last, gla(j))
        xl = xl + mod[:, :, 2] * yl
        xl = xl + mod[:, :, 5] * peer(modulate(xl, norm_ffn[i], mod[:, :, 3], mod[:, :, 4]), pe(i))
        if not last:
            xc = xc + mod_c[:, :, 2] * yc
            xc = xc + mod_c[:, :, 5] * peer(modulate(xc, norm_ffn[i], mod_c[:, :, 3], mod_c[:, :, 4]), pe(i))
    return _final_norm(xl, norm_out)
```

```python
import math
import functools
import jax, jax.numpy as jnp
from jax import lax
from jax.experimental import pallas as pl
from jax.experimental.pallas import tpu as pltpu

D_MODEL = 1024
BATCH = 8
SEQ = 8192
DEPTH = 4

GRID_W = 64
CTX_LEN = 256
N_MIXERS = 3
NORM_EPS = 1e-6

RW_HEAD = 64
RW_HEADS = D_MODEL // RW_HEAD
RW_GN_EPS = 64e-5

GLA_HEADS = 4
GLA_DK = D_MODEL // 2 // GLA_HEADS
GLA_DV = D_MODEL // GLA_HEADS
GLA_GATE_NORM = 16.0
GLA_CHUNK = 64

HY_ORDER = 2
HY_EMB = 33
HY_BANDS = (HY_EMB - 1) // 2
HY_FAST_DECAY = 0.3
HY_SLOW_DECAY = 1.5
HY_TARGET = 1e-2

PEER_HEADS = 8
PEER_NKEYS = 128
PEER_QDIM = 256
PEER_HALF = PEER_QDIM // 2
PEER_TOPK = 16
PEER_BLOCK = 128


def _final_norm_kernel(x_ref, g_ref, o_ref):
    x = x_ref[...]
    y = x * lax.rsqrt(jnp.mean(x * x, -1, keepdims=True) + NORM_EPS)
    o_ref[...] = y * g_ref[...]


def _final_norm(x, g):
    b_, l, d = x.shape
    rows = b_ * l
    tile = 1024
    out = pl.pallas_call(
        _final_norm_kernel,
        grid=(rows // tile,),
        in_specs=[pl.BlockSpec((tile, d), lambda i: (i, 0)),
                  pl.BlockSpec((1, d), lambda i: (0, 0))],
        out_specs=pl.BlockSpec((tile, d), lambda i: (i, 0)),
        out_shape=jax.ShapeDtypeStruct((rows, d), x.dtype),
        name="final_norm",
    )(x.reshape(rows, d), g.reshape(1, d))
    return out.reshape(b_, l, d)


RW_CHUNK = 64
RW_LANES = 128
RW_PAIRS_PER_STEP = 4
_HI = lax.Precision.HIGHEST


def _dot(x, y, precision=None):
    return jnp.dot(x, y, precision=precision, preferred_element_type=jnp.float32)


def _dot_nt(x, y, precision=None):
    return lax.dot_general(x, y, (((1,), (1,)), ((), ())), precision=precision,
                           preferred_element_type=jnp.float32)


def _dot_tn(x, y, precision=None):
    return lax.dot_general(x, y, (((0,), (0,)), ((), ())), precision=precision,
                           preferred_element_type=jnp.float32)


def _unit_triangular_inverses(mats):
    c = mats[0].shape[0]
    row = lax.broadcasted_iota(jnp.int32, (c, c), 0)
    col = lax.broadcasted_iota(jnp.int32, (c, c), 1)
    eye = jnp.where(row == col, 1.0, 0.0)
    ts = [eye + a for a in mats]
    ps = list(mats)
    steps = c.bit_length() - 2
    for _ in range(steps):
        ps = [_dot(p, p, _HI) for p in ps]
        ts = [t + _dot(t, p, _HI) for t, p in zip(ts, ps)]
    return ts


def _rwkv_chunk_pairs(seqs, ms, reverse):
    f32 = jnp.float32
    C = RW_CHUNK
    row = lax.broadcasted_iota(jnp.int32, (C, C), 0)
    col = lax.broadcasted_iota(jnp.int32, (C, C), 1)
    if reverse:
        incl, strict = row <= col, row < col
    else:
        incl, strict = row >= col, row > col
    tri = jnp.where(incl, 1.0, 0.0).astype(f32)
    lane = lax.broadcasted_iota(jnp.int32, (1, RW_LANES), 1)
    head_masks = [jnp.where(lane < RW_HEAD, 1.0, 0.0).astype(f32),
                  jnp.where(lane >= RW_HEAD, 1.0, 0.0).astype(f32)]
    ri = lax.broadcasted_iota(jnp.int32, (RW_LANES, RW_LANES), 0)
    ci = lax.broadcasted_iota(jnp.int32, (RW_LANES, RW_LANES), 1)
    same_head = (ri < RW_HEAD) == (ci < RW_HEAD)
    n = len(seqs)

    cums = [_dot(tri, s[1], _HI) for s in seqs]
    pre = []
    for (r, lw, k, v, kk, bb), cum in zip(seqs, cums):
        mid = cum[C // 2:C // 2 + 1, :]
        tot = cum[0:1, :] if reverse else cum[C - 1:C, :]
        e_inv = jnp.exp(mid - cum)
        e_out = jnp.exp(tot - cum)
        pre.append(dict(
            tot=tot, v=v,
            a_rel=-kk * jnp.exp(cum - lw - mid), b_rel=bb * e_inv, k_rel=k * e_inv,
            r_rel=r * jnp.exp(cum - mid), a_abs=-kk * jnp.exp(cum - lw), r_abs=r * jnp.exp(cum),
            k_out=k * e_out, b_out=bb * e_out))
    uns = [_dot(p["a_abs"], m) for p, m in zip(pre, ms)]
    yns = [_dot(p["r_abs"], m) for p, m in zip(pre, ms)]
    a_hs = [[p["a_rel"] * mh for mh in head_masks] for p in pre]
    a_abs_ = [[jnp.where(strict, _dot_nt(a_h, p["b_rel"]), 0.0) for a_h in a_hs[i]] for i, p in enumerate(pre)]
    a_aks = [[jnp.where(strict, _dot_nt(a_h, p["k_rel"]), 0.0) for a_h in a_hs[i]] for i, p in enumerate(pre)]
    rhss = [[uns[i] + _dot(a_ak, p["v"]) for a_ak in a_aks[i]] for i, p in enumerate(pre)]
    invs = _unit_triangular_inverses([a for pair in a_abs_ for a in pair])
    us = []
    for i in range(n):
        parts = [_dot(invs[2 * i + h], rhss[i][h], _HI) for h in range(2)]
        us.append(head_masks[0] * parts[0] + head_masks[1] * parts[1])
    r_hs = [[p["r_rel"] * mh for mh in head_masks] for p in pre]
    r_bs = [[jnp.where(incl, _dot_nt(r_h, p["b_rel"]), 0.0) for r_h in r_hs[i]] for i, p in enumerate(pre)]
    r_ks = [[jnp.where(incl, _dot_nt(r_h, p["k_rel"]), 0.0) for r_h in r_hs[i]] for i, p in enumerate(pre)]
    out = []
    for i, p in enumerate(pre):
        y = yns[i]
        for h, mh in enumerate(head_masks):
            y = y + mh * (_dot(r_bs[i][h], us[i]) + _dot(r_ks[i][h], p["v"]))
        decay = jnp.where(ri == ci, jnp.broadcast_to(jnp.exp(p["tot"]), (RW_LANES, RW_LANES)), 0.0)
        m_new = _dot(decay, ms[i], _HI) + jnp.where(
            same_head, _dot_tn(p["b_out"], us[i]) + _dot_tn(p["k_out"], p["v"]), 0.0)
        out.append((y, m_new))
    return out


def _rwkv_chunk_kernel(r_ref, lw_ref, k_ref, v_ref, kk_ref, bb_ref, m0_ref,
                       y_ref, mfin_ref, m_scr, *, reverse):
    c_idx = pl.program_id(2)
    n_chunks = pl.num_programs(2)

    @pl.when(c_idx == 0)
    def _():
        m_scr[...] = m0_ref[0]

    lanes = [slice(p * RW_LANES, (p + 1) * RW_LANES) for p in range(RW_PAIRS_PER_STEP)]
    seqs = [tuple(ref[0, :, sl] for ref in (r_ref, lw_ref, k_ref, v_ref, kk_ref, bb_ref)) for sl in lanes]
    results = _rwkv_chunk_pairs(seqs, [m_scr[p] for p in range(RW_PAIRS_PER_STEP)], reverse)
    for p, (sl, (y, m_new)) in enumerate(zip(lanes, results)):
        y_ref[0, :, sl] = y
        m_scr[p] = m_new

    @pl.when(c_idx == n_chunks - 1)
    def _():
        mfin_ref[0] = m_scr[...]


def rwkv_chunked(r, lw, k, v, kk, bb, s0, reverse, interpret=False):
    b_, l, d = r.shape
    n_pairs = d // RW_LANES
    n_chunks = l // RW_CHUNK
    hn = RW_HEAD
    st = jnp.swapaxes(s0, -1, -2).reshape(b_, n_pairs, 2, hn, hn)
    zero = jnp.zeros_like(st[:, :, 0])
    m0 = jnp.concatenate([jnp.concatenate([st[:, :, 0], zero], -1),
                          jnp.concatenate([zero, st[:, :, 1]], -1)], -2)
    if reverse:
        seq_map = lambda b, p, c: (b, n_chunks - 1 - c, p)
    else:
        seq_map = lambda b, p, c: (b, c, p)
    pps = RW_PAIRS_PER_STEP
    seq_spec = pl.BlockSpec((1, RW_CHUNK, pps * RW_LANES), seq_map)
    st_spec = pl.BlockSpec((1, pps, RW_LANES, RW_LANES), lambda b, p, c: (b, p, 0, 0))
    y, mfin = pl.pallas_call(
        functools.partial(_rwkv_chunk_kernel, reverse=reverse),
        grid=(b_, n_pairs // pps, n_chunks),
        in_specs=[seq_spec] * 6 + [st_spec],
        out_specs=[seq_spec, st_spec],
        out_shape=[jax.ShapeDtypeStruct((b_, l, d), jnp.float32),
                   jax.ShapeDtypeStruct((b_, n_pairs, RW_LANES, RW_LANES), jnp.float32)],
        scratch_shapes=[pltpu.VMEM((pps, RW_LANES, RW_LANES), jnp.float32)],
        compiler_params=pltpu.CompilerParams(
            dimension_semantics=("parallel", "parallel", "arbitrary")),
        name="rwkv_scan_bwd" if reverse else "rwkv_scan_fwd",
        interpret=interpret,
    )(r, lw, k, v, kk, bb, m0)
    s_fin = jnp.stack([mfin[:, :, :hn, :hn], mfin[:, :, hn:, hn:]], 2)
    return y, jnp.swapaxes(s_fin.reshape(b_, 2 * n_pairs, hn, hn), -1, -2)


PEER_TOK = 16
PEER_SEL = PEER_HEADS * PEER_TOPK
PEER_ROWS = PEER_TOK * PEER_SEL
SUBLANES = 8
LANES = 128
PEER_GROUPS = PEER_ROWS // SUBLANES
PEER_DTILES = D_MODEL // LANES


def _peer_issue_group(idx_ref, uv_hbm, buf, sem, slot, grp):
    for sub in range(SUBLANES):
        e = idx_ref[0, 0, grp * SUBLANES + sub]
        pltpu.make_async_copy(uv_hbm.at[e], buf.at[slot, grp, :, sub, :],
                              sem.at[slot]).start(priority=sub % 2)


def _peer_wait_slot(buf, sem, slot):
    pltpu.make_async_copy(buf.at[1 - slot], buf.at[slot], sem.at[slot]).wait()


def _peer_expert_kernel(idx_first_ref, idx_next_ref, h_ref, g_ref, uv_hbm, o_ref, buf, sem):
    i = pl.program_id(0)
    n = pl.num_programs(0)

    @pl.when(i == 0)
    def _():
        def body(grp, carry):
            _peer_issue_group(idx_first_ref, uv_hbm, buf, sem, 0, grp)
            return carry
        lax.fori_loop(0, PEER_GROUPS, body, 0)

    bf16 = jnp.bfloat16
    pieces = 2 * PEER_DTILES
    per_piece = PEER_GROUPS // pieces

    def step(slot):
        _peer_wait_slot(buf, sem, slot)

        def issue_piece(piece):
            for grp in range(piece * per_piece, (piece + 1) * per_piece):
                _peer_issue_group(idx_next_ref, uv_hbm, buf, sem, 1 - slot, grp)

        def half(c, low):
            x = buf[slot, :, c, :, :].reshape(PEER_ROWS, LANES)
            x = (x << 16) if low else (x & jnp.uint32(0xFFFF0000))
            return lax.bitcast_convert_type(x, jnp.float32).astype(bf16)

        h = h_ref[...].astype(bf16)
        scores = jnp.zeros((PEER_TOK, PEER_ROWS), jnp.float32)
        for c in range(PEER_DTILES):
            scores = scores + _dot_nt(h[:, c * LANES:(c + 1) * LANES], half(c, False))
            issue_piece(c)
        tok = lax.broadcasted_iota(jnp.int32, (PEER_TOK, PEER_ROWS), 0)
        owner = lax.broadcasted_iota(jnp.int32, (PEER_TOK, PEER_ROWS), 1) // PEER_SEL
        own = tok == owner
        s_row = jnp.sum(jnp.where(own, scores, 0.0), axis=0, keepdims=True)
        act = 0.5 * s_row * (1.0 + lax.erf(s_row * (2.0 ** -0.5))) * g_ref[0]
        a_mat = jnp.where(own, jnp.broadcast_to(act, (PEER_TOK, PEER_ROWS)), 0.0).astype(bf16)
        for c in range(PEER_DTILES):
            o_ref[:, c * LANES:(c + 1) * LANES] = _dot(a_mat, half(c, True))
            issue_piece(PEER_DTILES + c)

        @pl.when(i == n - 1)
        def _():
            _peer_wait_slot(buf, sem, 1 - slot)

    step(i % 2)


def pack_expert_table(u, v):
    bits = lambda t: lax.bitcast_convert_type(t.astype(jnp.bfloat16), jnp.uint16).astype(jnp.uint32)
    return ((bits(u) << 16) | bits(v)).reshape(u.shape[0], PEER_DTILES, LANES)


def peer_experts(h, e, g, uv, interpret=False):
    n, d = h.shape
    steps = n // PEER_TOK
    idx = e.reshape(steps, 1, PEER_ROWS)
    gate = g.reshape(steps, 1, PEER_ROWS)
    return pl.pallas_call(
        _peer_expert_kernel,
        grid=(steps,),
        in_specs=[
            pl.BlockSpec((1, 1, PEER_ROWS), lambda i: (0, 0, 0), memory_space=pltpu.SMEM),
            pl.BlockSpec((1, 1, PEER_ROWS), lambda i: (jnp.minimum(i + 1, steps - 1), 0, 0),
                         memory_space=pltpu.SMEM),
            pl.BlockSpec((PEER_TOK, d), lambda i: (i, 0)),
            pl.BlockSpec((1, 1, PEER_ROWS), lambda i: (i, 0, 0)),
            pl.BlockSpec(memory_space=pl.ANY),
        ],
        out_specs=pl.BlockSpec((PEER_TOK, d), lambda i: (i, 0)),
        out_shape=jax.ShapeDtypeStruct((n, d), jnp.float32),
        scratch_shapes=[pltpu.VMEM((2, PEER_GROUPS, PEER_DTILES, SUBLANES, LANES), jnp.uint32),
                        pltpu.SemaphoreType.DMA((2,))],
        compiler_params=pltpu.CompilerParams(
            dimension_semantics=("arbitrary",),
            vmem_limit_bytes=40 * 1024 * 1024),
        name="peer_experts",
        interpret=interpret,
    )(idx, idx, h, gate, uv)


ROUTE_TOK = 128


def _top_rounds(s, rounds):
    n_rows, t = s.shape
    row = lax.broadcasted_iota(jnp.int32, (n_rows, t), 0)
    out_row = lax.broadcasted_iota(jnp.int32, (rounds, t), 0)
    vals = jnp.zeros((rounds, t), jnp.float32)
    idxs = jnp.zeros((rounds, t), jnp.int32)
    for r in range(rounds):
        m = jnp.max(s, axis=0, keepdims=True)
        i = jnp.min(jnp.where(s == m, row, n_rows), axis=0, keepdims=True)
        vals = jnp.where(out_row == r, m, vals)
        idxs = jnp.where(out_row == r, i, idxs)
        s = jnp.where(row == i, -jnp.inf, s)
    return vals, idxs


def _select_rows(table, sel):
    k, t = table.shape
    out = jnp.zeros(sel.shape, table.dtype)
    for j in range(k):
        out = jnp.where(sel == j, table[j:j + 1, :], out)
    return out


def _peer_route_kernel(h_ref, wq_ref, keys_ref, e_ref, g_ref):
    q = _dot(h_ref[...], wq_ref[...])
    e_rows, g_rows = [], []
    for hd in range(PEER_HEADS):
        sv, si = [], []
        for p in range(2):
            c0 = (hd * 2 + p) * PEER_HALF
            s_t = _dot_nt(keys_ref[hd, p], q[:, c0:c0 + PEER_HALF])
            v_, i_ = _top_rounds(s_t, PEER_TOPK)
            sv.append(v_)
            si.append(i_)
        comb = jnp.concatenate([sv[0][a:a + 1, :] + sv[1] for a in range(PEER_TOPK)], axis=0)
        fv, fi = _top_rounds(comb, PEER_TOPK)
        i1 = _select_rows(si[0], fi // PEER_TOPK)
        i2 = _select_rows(si[1], fi % PEER_TOPK)
        e_rows.append(i1 * PEER_NKEYS + i2)
        ex = jnp.exp(fv - jnp.max(fv, axis=0, keepdims=True))
        g_rows.append(ex / jnp.sum(ex, axis=0, keepdims=True))
    e_ref[...] = jnp.concatenate(e_rows, axis=0).T
    g_ref[...] = jnp.concatenate(g_rows, axis=0).T


def peer_route(h, wq, keys, interpret=False):
    n, d = h.shape
    return pl.pallas_call(
        _peer_route_kernel,
        grid=(n // ROUTE_TOK,),
        in_specs=[pl.BlockSpec((ROUTE_TOK, d), lambda i: (i, 0)),
                  pl.BlockSpec(wq.shape, lambda i: (0, 0)),
                  pl.BlockSpec(keys.shape, lambda i: (0, 0, 0, 0))],
        out_specs=[pl.BlockSpec((ROUTE_TOK, PEER_SEL), lambda i: (i, 0)),
                   pl.BlockSpec((ROUTE_TOK, PEER_SEL), lambda i: (i, 0))],
        out_shape=[jax.ShapeDtypeStruct((n, PEER_SEL), jnp.int32),
                   jax.ShapeDtypeStruct((n, PEER_SEL), jnp.float32)],
        compiler_params=pltpu.CompilerParams(
            dimension_semantics=("parallel",), vmem_limit_bytes=40 * 1024 * 1024),
        name="peer_route",
        interpret=interpret,
    )(h, wq, keys)


def rmsnorm(x, g):
    xf = x.astype(jnp.float32)
    y = xf * lax.rsqrt(jnp.mean(xf * xf, -1, keepdims=True) + NORM_EPS)
    return y.astype(x.dtype) * g


def modulate(x, g, shift, scale):
    return rmsnorm(x, g) * (1 + scale) + shift


def shift_grid(h):
    b_, l, d = h.shape
    rows = l // GRID_W
    g = h.reshape(b_, rows, GRID_W, d)
    q = d // 4
    left = jnp.pad(g[:, :, :-1, :q], ((0, 0), (0, 0), (1, 0), (0, 0)))
    right = jnp.pad(g[:, :, 1:, q:2 * q], ((0, 0), (0, 0), (0, 1), (0, 0)))
    up = jnp.pad(g[:, :-1, :, 2 * q:3 * q], ((0, 0), (1, 0), (0, 0), (0, 0)))
    down = jnp.pad(g[:, 1:, :, 3 * q:], ((0, 0), (0, 1), (0, 0), (0, 0)))
    return jnp.concatenate([left, right, up, down], -1).reshape(b_, l, d)


def shift_seq(h):
    d = h.shape[-1] // 2
    prev = jnp.pad(h[:, :-1, :d], ((0, 0), (1, 0), (0, 0)))
    nxt = jnp.pad(h[:, 1:, d:], ((0, 0), (0, 1), (0, 0)))
    return jnp.concatenate([prev, nxt], -1)


def rwkv_mixer(hc, hl, need_ctx, p):
    (mu, w_rkv, w0, w1, w2, a0, a1, a2, g1, g2, k_k, k_a, r_k, ln_w, ln_b, w_o) = p
    f32 = jnp.float32

    def run(h, shifted, s_init):
        b_, l, d = h.shape
        heads = lambda t: t.reshape(b_, l, RW_HEADS, RW_HEAD).astype(f32)
        xx = shifted - h
        rkv = jnp.einsum('jbld,jde->jble', h[None] + xx[None] * mu[:3, None, None, :], w_rkv)
        r, k, v = rkv[0].astype(f32), rkv[1], rkv[2].astype(f32)
        xw, xa, xg = h + xx * mu[3], h + xx * mu[4], h + xx * mu[5]
        kk = heads(k * k_k)
        kk = (kk * lax.rsqrt(jnp.sum(kk * kk, -1, keepdims=True) + 1e-12)).reshape(b_, l, d)
        y = jnp.zeros_like(v)
        k_sum = jnp.zeros_like(v)
        finals = []
        for dr in range(2):
            wlog = -jax.nn.softplus(-(w0[dr] + jnp.tanh(xw @ w1[dr]) @ w2[dr]).astype(f32)) - 0.5
            a = jax.nn.sigmoid((a0[dr] + (xa @ a1[dr]) @ a2[dr]).astype(f32))
            kd = (k * (1 + (a - 1) * k_a)).astype(f32)
            ys, sf = rwkv_chunked(r, -jnp.exp(wlog), kd, v, kk, kk * a, s_init[dr], dr == 1)
            y = y + ys
            k_sum = k_sum + kd
            finals.append(sf)
        return (heads(y), heads(r), heads(v), heads(k_sum), xg), finals

    def post(h, y, r, v, k_sum, xg):
        b_, l, d = h.shape
        mean = jnp.mean(y, -1, keepdims=True)
        var = jnp.mean(jnp.square(y - mean), -1, keepdims=True)
        yn = ((y - mean) * lax.rsqrt(var + RW_GN_EPS)).reshape(b_, l, d) * ln_w + ln_b
        bonus = jnp.sum(r * k_sum * r_k.reshape(RW_HEADS, RW_HEAD).astype(f32), -1, keepdims=True) * v
        g = jax.nn.sigmoid(xg @ g1) @ g2
        return ((yn + bonus.reshape(b_, l, d)).astype(h.dtype) * g) @ w_o

    s_zero = jnp.zeros((hc.shape[0], RW_HEADS, RW_HEAD, RW_HEAD), f32)
    out_c, s_ctx = run(hc, shift_seq(hc), (s_zero, s_zero))
    out_l, _ = run(hl, shift_grid(hl), s_ctx)
    yc = post(hc, *out_c) if need_ctx else None
    return yc, post(hl, *out_l)


def _gla_chunk_kernel(qf_ref, kf_ref, vf_ref, ldf_ref, qb_ref, kb_ref, vb_ref, ldb_ref, s0f_ref, s0b_ref,
                      of_ref, ob_ref, sff_ref, sfb_ref, sf_scr, sb_scr):
    f32 = jnp.float32
    C = GLA_CHUNK
    c_idx = pl.program_id(2)
    n_chunks = pl.num_programs(2)

    @pl.when(c_idx == 0)
    def _():
        sf_scr[...] = s0f_ref[0, 0]
        sb_scr[...] = s0b_ref[0, 0]

    row = lax.broadcasted_iota(jnp.int32, (C, C), 0)
    col = lax.broadcasted_iota(jnp.int32, (C, C), 1)
    incls = [row >= col, row <= col]
    ri = lax.broadcasted_iota(jnp.int32, (GLA_DK, GLA_DK), 0)
    ci = lax.broadcasted_iota(jnp.int32, (GLA_DK, GLA_DK), 1)
    qs = [qf_ref[0], qb_ref[0]]
    ks = [kf_ref[0], kb_ref[0]]
    vs = [vf_ref[0], vb_ref[0]]
    lds = [ldf_ref[0], ldb_ref[0]]
    states = [sf_scr[...], sb_scr[...]]
    cums = [_dot(jnp.where(m, 1.0, 0.0).astype(f32), ld, _HI) for m, ld in zip(incls, lds)]
    mids = [cum[C // 2:C // 2 + 1, :] for cum in cums]
    lasts = [cums[0][C - 1:C, :], cums[1][0:1, :]]
    scores = [jnp.where(m, _dot_nt(q * jnp.exp(cum - mid), k * jnp.exp(mid - cum)), 0.0)
              for m, q, k, cum, mid in zip(incls, qs, ks, cums, mids)]
    outs = [_dot(sc, v) + _dot(q * jnp.exp(cum), s)
            for sc, v, q, cum, s in zip(scores, vs, qs, cums, states)]
    decays = [jnp.where(ri == ci, jnp.broadcast_to(jnp.exp(last), (GLA_DK, GLA_DK)), 0.0) for last in lasts]
    new_states = [_dot(dec, s, _HI) + _dot_tn(k * jnp.exp(last - cum), v)
                  for dec, s, k, last, cum, v in zip(decays, states, ks, lasts, cums, vs)]
    of_ref[0] = outs[0]
    ob_ref[0] = outs[1]
    sf_scr[...] = new_states[0]
    sb_scr[...] = new_states[1]

    @pl.when(c_idx == n_chunks - 1)
    def _():
        sff_ref[0, 0] = new_states[0]
        sfb_ref[0, 0] = new_states[1]


def gla_scan(q, k, v, ld_f, ld_b, s0_f, s0_b, interpret=False):
    b_, l, _ = q.shape
    n = l // GLA_CHUNK
    fwd = lambda b, h, c: (b, c, h)
    bwd = lambda b, h, c: (b, n - 1 - c, h)
    kspec = lambda m: pl.BlockSpec((1, GLA_CHUNK, GLA_DK), m)
    vspec = lambda m: pl.BlockSpec((1, GLA_CHUNK, GLA_DV), m)
    sspec = pl.BlockSpec((1, 1, GLA_DK, GLA_DV), lambda b, h, c: (b, h, 0, 0))
    o_f, o_b, sf, sb = pl.pallas_call(
        _gla_chunk_kernel,
        grid=(b_, GLA_HEADS, n),
        in_specs=[kspec(fwd), kspec(fwd), vspec(fwd), kspec(fwd),
                  kspec(bwd), kspec(bwd), vspec(bwd), kspec(bwd), sspec, sspec],
        out_specs=[vspec(fwd), vspec(bwd), sspec, sspec],
        out_shape=[jax.ShapeDtypeStruct(v.shape, jnp.float32), jax.ShapeDtypeStruct(v.shape, jnp.float32),
                   jax.ShapeDtypeStruct(s0_f.shape, jnp.float32), jax.ShapeDtypeStruct(s0_b.shape, jnp.float32)],
        scratch_shapes=[pltpu.VMEM((GLA_DK, GLA_DV), jnp.float32), pltpu.VMEM((GLA_DK, GLA_DV), jnp.float32)],
        compiler_params=pltpu.CompilerParams(
            dimension_semantics=("parallel", "parallel", "arbitrary")),
        name="gla_scan",
        interpret=interpret,
    )(q, k, v, ld_f, q, k, v, ld_b, s0_f, s0_b)
    return o_f + o_b, sf, sb


def gla_mixer(hc, hl, need_ctx, p):
    w_in, wg1, wg2, bg, norm_w, w_o = p
    f32 = jnp.float32
    hk = GLA_HEADS * GLA_DK

    def run(h, s_init):
        b_, l, d = h.shape
        q, k, v, og = jnp.split(h @ w_in, [hk, 2 * hk, 2 * hk + d], -1)
        q = q.astype(f32) * GLA_DK ** -0.5
        logd = [jax.nn.log_sigmoid(((h @ wg1[dr]) @ wg2[dr] + bg[dr]).astype(f32)) / GLA_GATE_NORM
                for dr in range(2)]
        o, s_f, s_b = gla_scan(q, k.astype(f32), v.astype(f32), logd[0], logd[1], s_init[0], s_init[1])
        return o.reshape(b_, l, GLA_HEADS, GLA_DV), og, (s_f, s_b)

    def post(h, o, og):
        b_, l, d = h.shape
        on = o * lax.rsqrt(jnp.mean(o * o, -1, keepdims=True) + NORM_EPS) * norm_w
        return (on.reshape(b_, l, d).astype(h.dtype) * jax.nn.silu(og)) @ w_o

    s_zero = jnp.zeros((hc.shape[0], GLA_HEADS, GLA_DK, GLA_DV), f32)
    oc, ogc, s_ctx = run(hc, (s_zero, s_zero))
    ol, ogl, _ = run(hl, s_ctx)
    yc = post(hc, oc, ogc) if need_ctx else None
    return yc, post(hl, ol, ogl)


def conv3(z, w, b):
    zp = jnp.pad(z, ((0, 0), (1, 1), (0, 0)))
    return zp[:, :-2] * w[0] + zp[:, 1:-1] * w[1] + zp[:, 2:] * w[2] + b


def hyena_filters(l, f1, fb1, f2, fb2, f3, fb3, f4, freq):
    f32 = jnp.float32
    t = jnp.linspace(0.0, 1.0, l, dtype=f32)[:, None]
    w = 2 * math.pi * jnp.arange(l, dtype=f32)[:, None] / l
    f = jnp.linspace(1e-4, HY_BANDS - 1, HY_BANDS, dtype=f32)[None]
    z = jnp.concatenate([t, jnp.cos(f * w), -jnp.sin(f * w)], -1)
    fr = freq.astype(f32)
    hdn = jnp.sin(fr * (z @ f1.astype(f32) + fb1.astype(f32)))
    hdn = jnp.sin(fr * (hdn @ f2.astype(f32) + fb2.astype(f32)))
    hdn = jnp.sin(fr * (hdn @ f3.astype(f32) + fb3.astype(f32)))
    filt = (hdn @ f4.astype(f32)).reshape(l, HY_ORDER, 2, D_MODEL)
    min_decay = math.log(HY_TARGET) / HY_SLOW_DECAY
    max_decay = math.log(HY_TARGET) / HY_FAST_DECAY
    deltas = jnp.linspace(min_decay, max_decay, D_MODEL, dtype=f32)
    filt = filt * jnp.exp(-t * jnp.abs(deltas))[:, None, None, :]
    fwd, bwd = filt[:, :, 0], filt[:, :, 1]
    kc = jnp.concatenate([fwd, jnp.zeros_like(fwd[:1]), jnp.flip(bwd[1:], 0)], 0)
    return kc / jnp.sum(jnp.abs(kc), 0, keepdims=True)


def long_conv(u, kc_o, bias_o):
    l = u.shape[1]
    uf = jnp.fft.rfft(u, n=2 * l, axis=1)
    kf = jnp.fft.rfft(kc_o, n=2 * l, axis=0)
    return jnp.fft.irfft(uf * kf[None], n=2 * l, axis=1)[:, :l] + u * bias_o


def hyena_mixer(h, p):
    (w_in, b_in, sc_w, sc_b, f1, fb1, f2, fb2, f3, fb3, f4, freq, bias, w_o, b_o) = p
    l = h.shape[1]
    z = conv3(h @ w_in + b_in, sc_w, sc_b).astype(jnp.float32)
    v, x1, x2 = jnp.split(z, 3, -1)
    kc = hyena_filters(l, f1, fb1, f2, fb2, f3, fb3, f4, freq)
    bias = bias.astype(jnp.float32)
    y = x1 * long_conv(v, kc[:, 0], bias[0])
    y = x2 * long_conv(y, kc[:, 1], bias[1])
    return y.astype(h.dtype) @ w_o + b_o


def peer(h, p):
    wq, keys, uv = p
    b_, l, d = h.shape
    n = b_ * l
    hf = h.reshape(n, d)
    e, g = peer_route(hf, wq, keys)
    return peer_experts(hf, e, g, uv).reshape(b_, l, d)


def kernel(x, c, ctx, c_ctx, ada_w, ada_b, norm_mix, norm_ffn, norm_out,
           rw_mu, rw_wrkv, rw_w0, rw_w1, rw_w2, rw_a0, rw_a1, rw_a2, rw_g1, rw_g2,
           rw_kk, rw_ka, rw_rk, rw_lnw, rw_lnb, rw_wo,
           gla_win, gla_wg1, gla_wg2, gla_bg, gla_normw, gla_wo,
           hy_win, hy_bin, hy_scw, hy_scb, hy_f1, hy_fb1, hy_f2, hy_fb2, hy_f3, hy_fb3,
           hy_f4, hy_freq, hy_bias, hy_wo, hy_bo,
           pe_wq, pe_keys, pe_u, pe_v):
    rw = lambda j: (rw_mu[j], rw_wrkv[j], rw_w0[j], rw_w1[j], rw_w2[j], rw_a0[j], rw_a1[j], rw_a2[j],
                    rw_g1[j], rw_g2[j], rw_kk[j], rw_ka[j], rw_rk[j], rw_lnw[j], rw_lnb[j], rw_wo[j])
    gla = lambda j: (gla_win[j], gla_wg1[j], gla_wg2[j], gla_bg[j], gla_normw[j], gla_wo[j])
    hy = lambda j: (hy_win[j], hy_bin[j], hy_scw[j], hy_scb[j], hy_f1[j], hy_fb1[j], hy_f2[j], hy_fb2[j],
                    hy_f3[j], hy_fb3[j], hy_f4[j], hy_freq[j], hy_bias[j], hy_wo[j], hy_bo[j])
    pe_uv = [pack_expert_table(pe_u[i], pe_v[i]) for i in range(DEPTH)]
    pe = lambda i: (pe_wq[i], pe_keys[i], pe_uv[i])

    xl, xc = x, ctx
    cond = jax.nn.silu(c)
    cond_ctx = jax.nn.silu(c_ctx)
    for i in range(DEPTH):
        kind, j = i % N_MIXERS, i // N_MIXERS
        last = i == DEPTH - 1
        mod = (cond @ ada_w[i] + ada_b[i]).reshape(-1, 1, 6, D_MODEL)
        mod_c = (cond_ctx @ ada_w[i] + ada_b[i]).reshape(1, 1, 6, D_MODEL)
        hl = modulate(xl, norm_mix[i], mod[:, :, 0], mod[:, :, 1])
        if kind == 2:
            yl = hyena_mixer(hl, hy(j))
            yc = None if last else hyena_mixer(modulate(xc, norm_mix[i], mod_c[:, :, 0], mod_c[:, :, 1]), hy(j))
        else:
            hc = modulate(xc, norm_mix[i], mod_c[:, :, 0], mod_c[:, :, 1])
            if kind == 0:
                yc, yl = rwkv_mixer(hc, hl, not last, rw(j))
            else:
                yc, yl = gla_mixer(hc, hl, not last, gla(j))
        xl = xl + mod[:, :, 2] * yl
        xl = xl + mod[:, :, 5] * peer(modulate(xl, norm_ffn[i], mod[:, :, 3], mod[:, :, 4]), pe(i))
        if not last:
            xc = xc + mod_c[:, :, 2] * yc
            xc = xc + mod_c[:, :, 5] * peer(modulate(xc, norm_ffn[i], mod_c[:, :, 3], mod_c[:, :, 4]), pe(i))
    return _final_norm(xl, norm_out)
```

```python
import math
import functools
import jax, jax.numpy as jnp
from jax import lax
from jax.experimental import pallas as pl
from jax.experimental.pallas import tpu as pltpu

D_MODEL = 1024
BATCH = 8
SEQ = 8192
DEPTH = 4

GRID_W = 64
CTX_LEN = 256
N_MIXERS = 3
NORM_EPS = 1e-6

RW_HEAD = 64
RW_HEADS = D_MODEL // RW_HEAD
RW_GN_EPS = 64e-5

GLA_HEADS = 4
GLA_DK = D_MODEL // 2 // GLA_HEADS
GLA_DV = D_MODEL // GLA_HEADS
GLA_GATE_NORM = 16.0
GLA_CHUNK = 64

HY_ORDER = 2
HY_EMB = 33
HY_BANDS = (HY_EMB - 1) // 2
HY_FAST_DECAY = 0.3
HY_SLOW_DECAY = 1.5
HY_TARGET = 1e-2

PEER_HEADS = 8
PEER_NKEYS = 128
PEER_QDIM = 256
PEER_HALF = PEER_QDIM // 2
PEER_TOPK = 16
PEER_BLOCK = 128


def _final_norm_kernel(x_ref, g_ref, o_ref):
    x = x_ref[...]
    y = x * lax.rsqrt(jnp.mean(x * x, -1, keepdims=True) + NORM_EPS)
    o_ref[...] = y * g_ref[...]


def _final_norm(x, g):
    b_, l, d = x.shape
    rows = b_ * l
    tile = 1024
    out = pl.pallas_call(
        _final_norm_kernel,
        grid=(rows // tile,),
        in_specs=[pl.BlockSpec((tile, d), lambda i: (i, 0)),
                  pl.BlockSpec((1, d), lambda i: (0, 0))],
        out_specs=pl.BlockSpec((tile, d), lambda i: (i, 0)),
        out_shape=jax.ShapeDtypeStruct((rows, d), x.dtype),
        name="final_norm",
    )(x.reshape(rows, d), g.reshape(1, d))
    return out.reshape(b_, l, d)


RW_CHUNK = 64
RW_LANES = 128
RW_PAIRS_PER_STEP = 4
_HI = lax.Precision.HIGHEST


def _dot(x, y, precision=None):
    return jnp.dot(x, y, precision=precision, preferred_element_type=jnp.float32)


def _dot_nt(x, y, precision=None):
    return lax.dot_general(x, y, (((1,), (1,)), ((), ())), precision=precision,
                           preferred_element_type=jnp.float32)


def _dot_tn(x, y, precision=None):
    return lax.dot_general(x, y, (((0,), (0,)), ((), ())), precision=precision,
                           preferred_element_type=jnp.float32)


def _unit_triangular_inverses(mats):
    c = mats[0].shape[0]
    row = lax.broadcasted_iota(jnp.int32, (c, c), 0)
    col = lax.broadcasted_iota(jnp.int32, (c, c), 1)
    eye = jnp.where(row == col, 1.0, 0.0)
    ts = [eye + a for a in mats]
    ps = list(mats)
    steps = c.bit_length() - 2
    for _ in range(steps):
        ps = [_dot(p, p, _HI) for p in ps]
        ts = [t + _dot(t, p, _HI) for t, p in zip(ts, ps)]
    return ts


def _rwkv_chunk_pairs(seqs, ms, reverse):
    f32 = jnp.float32
    C = RW_CHUNK
    row = lax.broadcasted_iota(jnp.int32, (C, C), 0)
    col = lax.broadcasted_iota(jnp.int32, (C, C), 1)
    if reverse:
        incl, strict = row <= col, row < col
    else:
        incl, strict = row >= col, row > col
    tri = jnp.where(incl, 1.0, 0.0).astype(f32)
    lane = lax.broadcasted_iota(jnp.int32, (1, RW_LANES), 1)
    head_masks = [jnp.where(lane < RW_HEAD, 1.0, 0.0).astype(f32),
                  jnp.where(lane >= RW_HEAD, 1.0, 0.0).astype(f32)]
    ri = lax.broadcasted_iota(jnp.int32, (RW_LANES, RW_LANES), 0)
    ci = lax.broadcasted_iota(jnp.int32, (RW_LANES, RW_LANES), 1)
    same_head = (ri < RW_HEAD) == (ci < RW_HEAD)
    n = len(seqs)

    cums = [_dot(tri, s[1], _HI) for s in seqs]
    pre = []
    for (r, lw, k, v, kk, bb), cum in zip(seqs, cums):
        mid = cum[C // 2:C // 2 + 1, :]
        tot = cum[0:1, :] if reverse else cum[C - 1:C, :]
        e_inv = jnp.exp(mid - cum)
        e_out = jnp.exp(tot - cum)
        pre.append(dict(
            tot=tot, v=v,
            a_rel=-kk * jnp.exp(cum - lw - mid), b_rel=bb * e_inv, k_rel=k * e_inv,
            r_rel=r * jnp.exp(cum - mid), a_abs=-kk * jnp.exp(cum - lw), r_abs=r * jnp.exp(cum),
            k_out=k * e_out, b_out=bb * e_out))
    uns = [_dot(p["a_abs"], m) for p, m in zip(pre, ms)]
    yns = [_dot(p["r_abs"], m) for p, m in zip(pre, ms)]
    a_hs = [[p["a_rel"] * mh for mh in head_masks] for p in pre]
    a_abs_ = [[jnp.where(strict, _dot_nt(a_h, p["b_rel"]), 0.0) for a_h in a_hs[i]] for i, p in enumerate(pre)]
    a_aks = [[jnp.where(strict, _dot_nt(a_h, p["k_rel"]), 0.0) for a_h in a_hs[i]] for i, p in enumerate(pre)]
    rhss = [[uns[i] + _dot(a_ak, p["v"]) for a_ak in a_aks[i]] for i, p in enumerate(pre)]
    invs = _unit_triangular_inverses([a for pair in a_abs_ for a in pair])
    us = []
    for i in range(n):
        parts = [_dot(invs[2 * i + h], rhss[i][h], _HI) for h in range(2)]
        us.append(head_masks[0] * parts[0] + head_masks[1] * parts[1])
    r_hs = [[p["r_rel"] * mh for mh in head_masks] for p in pre]
    r_bs = [[jnp.where(incl, _dot_nt(r_h, p["b_rel"]), 0.0) for r_h in r_hs[i]] for i, p in enumerate(pre)]
    r_ks = [[jnp.where(incl, _dot_nt(r_h, p["k_rel"]), 0.0) for r_h in r_hs[i]] for i, p in enumerate(pre)]
    out = []
    for i, p in enumerate(pre):
        y = yns[i]
        for h, mh in enumerate(head_masks):
            y = y + mh * (_dot(r_bs[i][h], us[i]) + _dot(r_ks[i][h], p["v"]))
        decay = jnp.where(ri == ci, jnp.broadcast_to(jnp.exp(p["tot"]), (RW_LANES, RW_LANES)), 0.0)
        m_new = _dot(decay, ms[i], _HI) + jnp.where(
            same_head, _dot_tn(p["b_out"], us[i]) + _dot_tn(p["k_out"], p["v"]), 0.0)
        out.append((y, m_new))
    return out


def _rwkv_chunk_kernel(r_ref, lw_ref, k_ref, v_ref, kk_ref, bb_ref, m0_ref,
                       y_ref, mfin_ref, m_scr, *, reverse):
    c_idx = pl.program_id(2)
    n_chunks = pl.num_programs(2)

    @pl.when(c_idx == 0)
    def _():
        m_scr[...] = m0_ref[0]

    lanes = [slice(p * RW_LANES, (p + 1) * RW_LANES) for p in range(RW_PAIRS_PER_STEP)]
    seqs = [tuple(ref[0, :, sl] for ref in (r_ref, lw_ref, k_ref, v_ref, kk_ref, bb_ref)) for sl in lanes]
    results = _rwkv_chunk_pairs(seqs, [m_scr[p] for p in range(RW_PAIRS_PER_STEP)], reverse)
    for p, (sl, (y, m_new)) in enumerate(zip(lanes, results)):
        y_ref[0, :, sl] = y
        m_scr[p] = m_new

    @pl.when(c_idx == n_chunks - 1)
    def _():
        mfin_ref[0] = m_scr[...]


def rwkv_chunked(r, lw, k, v, kk, bb, s0, reverse, interpret=False):
    b_, l, d = r.shape
    n_pairs = d // RW_LANES
    n_chunks = l // RW_CHUNK
    hn = RW_HEAD
    st = jnp.swapaxes(s0, -1, -2).reshape(b_, n_pairs, 2, hn, hn)
    zero = jnp.zeros_like(st[:, :, 0])
    m0 = jnp.concatenate([jnp.concatenate([st[:, :, 0], zero], -1),
                          jnp.concatenate([zero, st[:, :, 1]], -1)], -2)
    if reverse:
        seq_map = lambda b, p, c: (b, n_chunks - 1 - c, p)
    else:
        seq_map = lambda b, p, c: (b, c, p)
    pps = RW_PAIRS_PER_STEP
    seq_spec = pl.BlockSpec((1, RW_CHUNK, pps * RW_LANES), seq_map)
    st_spec = pl.BlockSpec((1, pps, RW_LANES, RW_LANES), lambda b, p, c: (b, p, 0, 0))
    y, mfin = pl.pallas_call(
        functools.partial(_rwkv_chunk_kernel, reverse=reverse),
        grid=(b_, n_pairs // pps, n_chunks),
        in_specs=[seq_spec] * 6 + [st_spec],
        out_specs=[seq_spec, st_spec],
        out_shape=[jax.ShapeDtypeStruct((b_, l, d), jnp.float32),
                   jax.ShapeDtypeStruct((b_, n_pairs, RW_LANES, RW_LANES), jnp.float32)],
        scratch_shapes=[pltpu.VMEM((pps, RW_LANES, RW_LANES), jnp.float32)],
        compiler_params=pltpu.CompilerParams(
            dimension_semantics=("parallel", "parallel", "arbitrary")),
        name="rwkv_scan_bwd" if reverse else "rwkv_scan_fwd",
        interpret=interpret,
    )(r, lw, k, v, kk, bb, m0)
    s_fin = jnp.stack([mfin[:, :, :hn, :hn], mfin[:, :, hn:, hn:]], 2)
    return y, jnp.swapaxes(s_fin.reshape(b_, 2 * n_pairs, hn, hn), -1, -2)


PEER_TOK = 16
PEER_SEL = PEER_HEADS * PEER_TOPK
PEER_ROWS = PEER_TOK * PEER_SEL
SUBLANES = 8
LANES = 128
PEER_GROUPS = PEER_ROWS // SUBLANES
PEER_DTILES = D_MODEL // LANES
PEER_RB = 128


def _peer_issue_group(idx_ref, uv_hbm, buf, sem, slot, grp):
    for sub in range(SUBLANES):
        e = idx_ref[0, 0, grp * SUBLANES + sub]
        pltpu.make_async_copy(uv_hbm.at[e], buf.at[slot, grp, :, sub, :],
                              sem.at[slot]).start(priority=sub % 2)


def _peer_wait_slot(buf, sem, slot):
    pltpu.make_async_copy(buf.at[1 - slot], buf.at[slot], sem.at[slot]).wait()


def _peer_expert_kernel(idx_first_ref, idx_next_ref, h_ref, g_ref, uv_hbm, o_ref, buf, sem):
    i = pl.program_id(0)
    n = pl.num_programs(0)

    @pl.when(i == 0)
    def _():
        def body(grp, carry):
            _peer_issue_group(idx_first_ref, uv_hbm, buf, sem, 0, grp)
            return carry
        lax.fori_loop(0, PEER_GROUPS, body, 0)

    bf16 = jnp.bfloat16
    n_rb = PEER_ROWS // PEER_RB
    grp_rb = PEER_RB // SUBLANES
    pieces = 2 * PEER_DTILES * n_rb
    per_piece = PEER_GROUPS // pieces

    def step(slot):
        _peer_wait_slot(buf, sem, slot)

        def issue_piece(piece):
            for grp in range(piece * per_piece, (piece + 1) * per_piece):
                _peer_issue_group(idx_next_ref, uv_hbm, buf, sem, 1 - slot, grp)

        def half(rb, c, low):
            x = buf[slot, rb * grp_rb:(rb + 1) * grp_rb, c, :, :].reshape(PEER_RB, LANES)
            x = (x << 16) if low else (x & jnp.uint32(0xFFFF0000))
            return lax.bitcast_convert_type(x, jnp.float32).astype(bf16)

        h = h_ref[...].astype(bf16)
        piece = 0
        blocks = []
        for rb in range(n_rb):
            sc = jnp.zeros((PEER_TOK, PEER_RB), jnp.float32)
            for c in range(PEER_DTILES):
                sc = sc + _dot_nt(h[:, c * LANES:(c + 1) * LANES], half(rb, c, False))
                issue_piece(piece)
                piece += 1
            blocks.append(sc)
        scores = jnp.concatenate(blocks, axis=1)
        tok = lax.broadcasted_iota(jnp.int32, (PEER_TOK, PEER_ROWS), 0)
        owner = lax.broadcasted_iota(jnp.int32, (PEER_TOK, PEER_ROWS), 1) // PEER_SEL
        own = tok == owner
        s_row = jnp.sum(jnp.where(own, scores, 0.0), axis=0, keepdims=True)
        act = 0.5 * s_row * (1.0 + lax.erf(s_row * (2.0 ** -0.5))) * g_ref[0]
        a_mat = jnp.where(own, jnp.broadcast_to(act, (PEER_TOK, PEER_ROWS)), 0.0).astype(bf16)
        for c in range(PEER_DTILES):
            acc = jnp.zeros((PEER_TOK, LANES), jnp.float32)
            for rb in range(n_rb):
                acc = acc + _dot(a_mat[:, rb * PEER_RB:(rb + 1) * PEER_RB], half(rb, c, True))
                issue_piece(piece)
                piece += 1
            o_ref[:, c * LANES:(c + 1) * LANES] = acc

        @pl.when(i == n - 1)
        def _():
            _peer_wait_slot(buf, sem, 1 - slot)

    step(i % 2)


def pack_expert_table(u, v):
    bits = lambda t: lax.bitcast_convert_type(t.astype(jnp.bfloat16), jnp.uint16).astype(jnp.uint32)
    return ((bits(u) << 16) | bits(v)).reshape(u.shape[0], PEER_DTILES, LANES)


def peer_experts(h, e, g, uv, interpret=False):
    n, d = h.shape
    steps = n // PEER_TOK
    idx = e.reshape(steps, 1, PEER_ROWS)
    gate = g.reshape(steps, 1, PEER_ROWS)
    return pl.pallas_call(
        _peer_expert_kernel,
        grid=(steps,),
        in_specs=[
            pl.BlockSpec((1, 1, PEER_ROWS), lambda i: (0, 0, 0), memory_space=pltpu.SMEM),
            pl.BlockSpec((1, 1, PEER_ROWS), lambda i: (jnp.minimum(i + 1, steps - 1), 0, 0),
                         memory_space=pltpu.SMEM),
            pl.BlockSpec((PEER_TOK, d), lambda i: (i, 0)),
            pl.BlockSpec((1, 1, PEER_ROWS), lambda i: (i, 0, 0)),
            pl.BlockSpec(memory_space=pl.ANY),
        ],
        out_specs=pl.BlockSpec((PEER_TOK, d), lambda i: (i, 0)),
        out_shape=jax.ShapeDtypeStruct((n, d), jnp.float32),
        scratch_shapes=[pltpu.VMEM((2, PEER_GROUPS, PEER_DTILES, SUBLANES, LANES), jnp.uint32),
                        pltpu.SemaphoreType.DMA((2,))],
        compiler_params=pltpu.CompilerParams(
            dimension_semantics=("arbitrary",),
            vmem_limit_bytes=40 * 1024 * 1024),
        name="peer_experts",
        interpret=interpret,
    )(idx, idx, h, gate, uv)


ROUTE_TOK = 128


def _top_rounds(s, rounds):
    n_rows, t = s.shape
    row = lax.broadcasted_iota(jnp.int32, (n_rows, t), 0)
    out_row = lax.broadcasted_iota(jnp.int32, (rounds, t), 0)
    vals = jnp.zeros((rounds, t), jnp.float32)
    idxs = jnp.zeros((rounds, t), jnp.int32)
    for r in range(rounds):
        m = jnp.max(s, axis=0, keepdims=True)
        i = jnp.min(jnp.where(s == m, row, n_rows), axis=0, keepdims=True)
        vals = jnp.where(out_row == r, m, vals)
        idxs = jnp.where(out_row == r, i, idxs)
        s = jnp.where(row == i, -jnp.inf, s)
    return vals, idxs


def _top_pair_sums(x, y):
    k, t = x.shape
    assert k == 16
    sub = lax.broadcasted_iota(jnp.int32, (SUBLANES, t), 0)
    blocks = [x[0:1, :] + y]
    for a in range(1, 8):
        blocks.append(jnp.where(sub < k // (a + 1), x[a:a + 1, :] + y[0:SUBLANES, :], -jnp.inf))
    blocks.append(x[8:16, :] + y[0:1, :])
    vals, r = _top_rounds(jnp.concatenate(blocks, axis=0), k)
    blk = r // SUBLANES
    a_idx = jnp.where(r < k, 0, jnp.where(blk < 9, blk - 1, r - 64))
    b_idx = jnp.where(r < k, r, jnp.where(blk < 9, r % SUBLANES, 0))
    return vals, a_idx, b_idx


def _select_rows(table, sel):
    k, t = table.shape
    out = jnp.zeros(sel.shape, table.dtype)
    for j in range(k):
        out = jnp.where(sel == j, table[j:j + 1, :], out)
    return out


def _peer_route_kernel(h_ref, wq_ref, keys_ref, e_ref, g_ref):
    q = _dot(h_ref[...], wq_ref[...])
    e_rows, g_rows = [], []
    for hd in range(PEER_HEADS):
        sv, si = [], []
        for p in range(2):
            c0 = (hd * 2 + p) * PEER_HALF
            s_t = _dot_nt(keys_ref[hd, p], q[:, c0:c0 + PEER_HALF])
            v_, i_ = _top_rounds(s_t, PEER_TOPK)
            sv.append(v_)
            si.append(i_)
        fv, fa, fb = _top_pair_sums(sv[0], sv[1])
        i1 = _select_rows(si[0], fa)
        i2 = _select_rows(si[1], fb)
        e_rows.append(i1 * PEER_NKEYS + i2)
        ex = jnp.exp(fv - jnp.max(fv, axis=0, keepdims=True))
        g_rows.append(ex / jnp.sum(ex, axis=0, keepdims=True))
    e_ref[...] = jnp.concatenate(e_rows, axis=0).T
    g_ref[...] = jnp.concatenate(g_rows, axis=0).T


def peer_route(h, wq, keys, interpret=False):
    n, d = h.shape
    return pl.pallas_call(
        _peer_route_kernel,
        grid=(n // ROUTE_TOK,),
        in_specs=[pl.BlockSpec((ROUTE_TOK, d), lambda i: (i, 0)),
                  pl.BlockSpec(wq.shape, lambda i: (0, 0)),
                  pl.BlockSpec(keys.shape, lambda i: (0, 0, 0, 0))],
        out_specs=[pl.BlockSpec((ROUTE_TOK, PEER_SEL), lambda i: (i, 0)),
                   pl.BlockSpec((ROUTE_TOK, PEER_SEL), lambda i: (i, 0))],
        out_shape=[jax.ShapeDtypeStruct((n, PEER_SEL), jnp.int32),
                   jax.ShapeDtypeStruct((n, PEER_SEL), jnp.float32)],
        compiler_params=pltpu.CompilerParams(
            dimension_semantics=("parallel",), vmem_limit_bytes=40 * 1024 * 1024),
        name="peer_route",
        interpret=interpret,
    )(h, wq, keys)


def rmsnorm(x, g):
    xf = x.astype(jnp.float32)
    y = xf * lax.rsqrt(jnp.mean(xf * xf, -1, keepdims=True) + NORM_EPS)
    return y.astype(x.dtype) * g


def modulate(x, g, shift, scale):
    return rmsnorm(x, g) * (1 + scale) + shift


def shift_grid(h):
    b_, l, d = h.shape
    rows = l // GRID_W
    g = h.reshape(b_, rows, GRID_W, d)
    q = d // 4
    left = jnp.pad(g[:, :, :-1, :q], ((0, 0), (0, 0), (1, 0), (0, 0)))
    right = jnp.pad(g[:, :, 1:, q:2 * q], ((0, 0), (0, 0), (0, 1), (0, 0)))
    up = jnp.pad(g[:, :-1, :, 2 * q:3 * q], ((0, 0), (1, 0), (0, 0), (0, 0)))
    down = jnp.pad(g[:, 1:, :, 3 * q:], ((0, 0), (0, 1), (0, 0), (0, 0)))
    return jnp.concatenate([left, right, up, down], -1).reshape(b_, l, d)


def shift_seq(h):
    d = h.shape[-1] // 2
    prev = jnp.pad(h[:, :-1, :d], ((0, 0), (1, 0), (0, 0)))
    nxt = jnp.pad(h[:, 1:, d:], ((0, 0), (0, 1), (0, 0)))
    return jnp.concatenate([prev, nxt], -1)


def rwkv_mixer(hc, hl, need_ctx, p):
    (mu, w_rkv, w0, w1, w2, a0, a1, a2, g1, g2, k_k, k_a, r_k, ln_w, ln_b, w_o) = p
    f32 = jnp.float32

    def run(h, shifted, s_init):
        b_, l, d = h.shape
        heads = lambda t: t.reshape(b_, l, RW_HEADS, RW_HEAD).astype(f32)
        xx = shifted - h
        rkv = jnp.einsum('jbld,jde->jble', h[None] + xx[None] * mu[:3, None, None, :], w_rkv)
        r, k, v = rkv[0].astype(f32), rkv[1], rkv[2].astype(f32)
        xw, xa, xg = h + xx * mu[3], h + xx * mu[4], h + xx * mu[5]
        kk = heads(k * k_k)
        kk = (kk * lax.rsqrt(jnp.sum(kk * kk, -1, keepdims=True) + 1e-12)).reshape(b_, l, d)
        y = jnp.zeros_like(v)
        k_sum = jnp.zeros_like(v)
        finals = []
        for dr in range(2):
            wlog = -jax.nn.softplus(-(w0[dr] + jnp.tanh(xw @ w1[dr]) @ w2[dr]).astype(f32)) - 0.5
            a = jax.nn.sigmoid((a0[dr] + (xa @ a1[dr]) @ a2[dr]).astype(f32))
            kd = (k * (1 + (a - 1) * k_a)).astype(f32)
            ys, sf = rwkv_chunked(r, -jnp.exp(wlog), kd, v, kk, kk * a, s_init[dr], dr == 1)
            y = y + ys
            k_sum = k_sum + kd
            finals.append(sf)
        return (heads(y), heads(r), heads(v), heads(k_sum), xg), finals

    def post(h, y, r, v, k_sum, xg):
        b_, l, d = h.shape
        mean = jnp.mean(y, -1, keepdims=True)
        var = jnp.mean(jnp.square(y - mean), -1, keepdims=True)
        yn = ((y - mean) * lax.rsqrt(var + RW_GN_EPS)).reshape(b_, l, d) * ln_w + ln_b
        bonus = jnp.sum(r * k_sum * r_k.reshape(RW_HEADS, RW_HEAD).astype(f32), -1, keepdims=True) * v
        g = jax.nn.sigmoid(xg @ g1) @ g2
        return ((yn + bonus.reshape(b_, l, d)).astype(h.dtype) * g) @ w_o

    s_zero = jnp.zeros((hc.shape[0], RW_HEADS, RW_HEAD, RW_HEAD), f32)
    out_c, s_ctx = run(hc, shift_seq(hc), (s_zero, s_zero))
    out_l, _ = run(hl, shift_grid(hl), s_ctx)
    yc = post(hc, *out_c) if need_ctx else None
    return yc, post(hl, *out_l)


def _gla_chunk_kernel(qf_ref, kf_ref, vf_ref, ldf_ref, qb_ref, kb_ref, vb_ref, ldb_ref, s0f_ref, s0b_ref,
                      of_ref, ob_ref, sff_ref, sfb_ref, sf_scr, sb_scr):
    f32 = jnp.float32
    C = GLA_CHUNK
    c_idx = pl.program_id(2)
    n_chunks = pl.num_programs(2)

    @pl.when(c_idx == 0)
    def _():
        sf_scr[...] = s0f_ref[0, 0]
        sb_scr[...] = s0b_ref[0, 0]

    row = lax.broadcasted_iota(jnp.int32, (C, C), 0)
    col = lax.broadcasted_iota(jnp.int32, (C, C), 1)
    incls = [row >= col, row <= col]
    ri = lax.broadcasted_iota(jnp.int32, (GLA_DK, GLA_DK), 0)
    ci = lax.broadcasted_iota(jnp.int32, (GLA_DK, GLA_DK), 1)
    qs = [qf_ref[0], qb_ref[0]]
    ks = [kf_ref[0], kb_ref[0]]
    vs = [vf_ref[0], vb_ref[0]]
    lds = [ldf_ref[0], ldb_ref[0]]
    states = [sf_scr[...], sb_scr[...]]
    cums = [_dot(jnp.where(m, 1.0, 0.0).astype(f32), ld, _HI) for m, ld in zip(incls, lds)]
    mids = [cum[C // 2:C // 2 + 1, :] for cum in cums]
    lasts = [cums[0][C - 1:C, :], cums[1][0:1, :]]
    scores = [jnp.where(m, _dot_nt(q * jnp.exp(cum - mid), k * jnp.exp(mid - cum)), 0.0)
              for m, q, k, cum, mid in zip(incls, qs, ks, cums, mids)]
    outs = [_dot(sc, v) + _dot(q * jnp.exp(cum), s)
            for sc, v, q, cum, s in zip(scores, vs, qs, cums, states)]
    decays = [jnp.where(ri == ci, jnp.broadcast_to(jnp.exp(last), (GLA_DK, GLA_DK)), 0.0) for last in lasts]
    new_states = [_dot(dec, s, _HI) + _dot_tn(k * jnp.exp(last - cum), v)
                  for dec, s, k, last, cum, v in zip(decays, states, ks, lasts, cums, vs)]
    of_ref[0] = outs[0]
    ob_ref[0] = outs[1]
    sf_scr[...] = new_states[0]
    sb_scr[...] = new_states[1]

    @pl.when(c_idx == n_chunks - 1)
    def _():
        sff_ref[0, 0] = new_states[0]
        sfb_ref[0, 0] = new_states[1]


def gla_scan(q, k, v, ld_f, ld_b, s0_f, s0_b, interpret=False):
    b_, l, _ = q.shape
    n = l // GLA_CHUNK
    fwd = lambda b, h, c: (b, c, h)
    bwd = lambda b, h, c: (b, n - 1 - c, h)
    kspec = lambda m: pl.BlockSpec((1, GLA_CHUNK, GLA_DK), m)
    vspec = lambda m: pl.BlockSpec((1, GLA_CHUNK, GLA_DV), m)
    sspec = pl.BlockSpec((1, 1, GLA_DK, GLA_DV), lambda b, h, c: (b, h, 0, 0))
    o_f, o_b, sf, sb = pl.pallas_call(
        _gla_chunk_kernel,
        grid=(b_, GLA_HEADS, n),
        in_specs=[kspec(fwd), kspec(fwd), vspec(fwd), kspec(fwd),
                  kspec(bwd), kspec(bwd), vspec(bwd), kspec(bwd), sspec, sspec],
        out_specs=[vspec(fwd), vspec(bwd), sspec, sspec],
        out_shape=[jax.ShapeDtypeStruct(v.shape, jnp.float32), jax.ShapeDtypeStruct(v.shape, jnp.float32),
                   jax.ShapeDtypeStruct(s0_f.shape, jnp.float32), jax.ShapeDtypeStruct(s0_b.shape, jnp.float32)],
        scratch_shapes=[pltpu.VMEM((GLA_DK, GLA_DV), jnp.float32), pltpu.VMEM((GLA_DK, GLA_DV), jnp.float32)],
        compiler_params=pltpu.CompilerParams(
            dimension_semantics=("parallel", "parallel", "arbitrary")),
        name="gla_scan",
        interpret=interpret,
    )(q, k, v, ld_f, q, k, v, ld_b, s0_f, s0_b)
    return o_f + o_b, sf, sb


def gla_mixer(hc, hl, need_ctx, p):
    w_in, wg1, wg2, bg, norm_w, w_o = p
    f32 = jnp.float32
    hk = GLA_HEADS * GLA_DK

    def run(h, s_init):
        b_, l, d = h.shape
        q, k, v, og = jnp.split(h @ w_in, [hk, 2 * hk, 2 * hk + d], -1)
        q = q.astype(f32) * GLA_DK ** -0.5
        logd = [jax.nn.log_sigmoid(((h @ wg1[dr]) @ wg2[dr] + bg[dr]).astype(f32)) / GLA_GATE_NORM
                for dr in range(2)]
        o, s_f, s_b = gla_scan(q, k.astype(f32), v.astype(f32), logd[0], logd[1], s_init[0], s_init[1])
        return o.reshape(b_, l, GLA_HEADS, GLA_DV), og, (s_f, s_b)

    def post(h, o, og):
        b_, l, d = h.shape
        on = o * lax.rsqrt(jnp.mean(o * o, -1, keepdims=True) + NORM_EPS) * norm_w
        return (on.reshape(b_, l, d).astype(h.dtype) * jax.nn.silu(og)) @ w_o

    s_zero = jnp.zeros((hc.shape[0], GLA_HEADS, GLA_DK, GLA_DV), f32)
    oc, ogc, s_ctx = run(hc, (s_zero, s_zero))
    ol, ogl, _ = run(hl, s_ctx)
    yc = post(hc, oc, ogc) if need_ctx else None
    return yc, post(hl, ol, ogl)


def conv3(z, w, b):
    zp = jnp.pad(z, ((0, 0), (1, 1), (0, 0)))
    return zp[:, :-2] * w[0] + zp[:, 1:-1] * w[1] + zp[:, 2:] * w[2] + b


def hyena_filters(l, f1, fb1, f2, fb2, f3, fb3, f4, freq):
    f32 = jnp.float32
    t = jnp.linspace(0.0, 1.0, l, dtype=f32)[:, None]
    w = 2 * math.pi * jnp.arange(l, dtype=f32)[:, None] / l
    f = jnp.linspace(1e-4, HY_BANDS - 1, HY_BANDS, dtype=f32)[None]
    z = jnp.concatenate([t, jnp.cos(f * w), -jnp.sin(f * w)], -1)
    fr = freq.astype(f32)
    hdn = jnp.sin(fr * (z @ f1.astype(f32) + fb1.astype(f32)))
    hdn = jnp.sin(fr * (hdn @ f2.astype(f32) + fb2.astype(f32)))
    hdn = jnp.sin(fr * (hdn @ f3.astype(f32) + fb3.astype(f32)))
    filt = (hdn @ f4.astype(f32)).reshape(l, HY_ORDER, 2, D_MODEL)
    min_decay = math.log(HY_TARGET) / HY_SLOW_DECAY
    max_decay = math.log(HY_TARGET) / HY_FAST_DECAY
    deltas = jnp.linspace(min_decay, max_decay, D_MODEL, dtype=f32)
    filt = filt * jnp.exp(-t * jnp.abs(deltas))[:, None, None, :]
    fwd, bwd = filt[:, :, 0], filt[:, :, 1]
    kc = jnp.concatenate([fwd, jnp.zeros_like(fwd[:1]), jnp.flip(bwd[1:], 0)], 0)
    return kc / jnp.sum(jnp.abs(kc), 0, keepdims=True)


def long_conv(u, kc_o, bias_o):
    b_, l, _ = u.shape
    half = b_ // 2
    z = lax.complex(u[:half], u[half:])
    kf = jnp.fft.fft(kc_o.astype(jnp.complex64), n=2 * l, axis=0)
    out = jnp.fft.ifft(jnp.fft.fft(z, n=2 * l, axis=1) * kf[None], n=2 * l, axis=1)[:, :l]
    return jnp.concatenate([jnp.real(out), jnp.imag(out)], axis=0) + u * bias_o


def hyena_mixer(h, p):
    (w_in, b_in, sc_w, sc_b, f1, fb1, f2, fb2, f3, fb3, f4, freq, bias, w_o, b_o) = p
    l = h.shape[1]
    z = conv3(h @ w_in + b_in, sc_w, sc_b).astype(jnp.float32)
    v, x1, x2 = jnp.split(z, 3, -1)
    kc = hyena_filters(l, f1, fb1, f2, fb2, f3, fb3, f4, freq)
    bias = bias.astype(jnp.float32)
    y = x1 * long_conv(v, kc[:, 0], bias[0])
    y = x2 * long_conv(y, kc[:, 1], bias[1])
    return y.astype(h.dtype) @ w_o + b_o


def peer(h, p):
    wq, keys, uv = p
    b_, l, d = h.shape
    n = b_ * l
    hf = h.reshape(n, d)
    e, g = peer_route(hf, wq, keys)
    return peer_experts(hf, e, g, uv).reshape(b_, l, d)


def kernel(x, c, ctx, c_ctx, ada_w, ada_b, norm_mix, norm_ffn, norm_out,
           rw_mu, rw_wrkv, rw_w0, rw_w1, rw_w2, rw_a0, rw_a1, rw_a2, rw_g1, rw_g2,
           rw_kk, rw_ka, rw_rk, rw_lnw, rw_lnb, rw_wo,
           gla_win, gla_wg1, gla_wg2, gla_bg, gla_normw, gla_wo,
           hy_win, hy_bin, hy_scw, hy_scb, hy_f1, hy_fb1, hy_f2, hy_fb2, hy_f3, hy_fb3,
           hy_f4, hy_freq, hy_bias, hy_wo, hy_bo,
           pe_wq, pe_keys, pe_u, pe_v):
    rw = lambda j: (rw_mu[j], rw_wrkv[j], rw_w0[j], rw_w1[j], rw_w2[j], rw_a0[j], rw_a1[j], rw_a2[j],
                    rw_g1[j], rw_g2[j], rw_kk[j], rw_ka[j], rw_rk[j], rw_lnw[j], rw_lnb[j], rw_wo[j])
    gla = lambda j: (gla_win[j], gla_wg1[j], gla_wg2[j], gla_bg[j], gla_normw[j], gla_wo[j])
    hy = lambda j: (hy_win[j], hy_bin[j], hy_scw[j], hy_scb[j], hy_f1[j], hy_fb1[j], hy_f2[j], hy_fb2[j],
                    hy_f3[j], hy_fb3[j], hy_f4[j], hy_freq[j], hy_bias[j], hy_wo[j], hy_bo[j])
    pe_uv = [pack_expert_table(pe_u[i], pe_v[i]) for i in range(DEPTH)]
    pe = lambda i: (pe_wq[i], pe_keys[i], pe_uv[i])

    xl, xc = x, ctx
    cond = jax.nn.silu(c)
    cond_ctx = jax.nn.silu(c_ctx)
    for i in range(DEPTH):
        kind, j = i % N_MIXERS, i // N_MIXERS
        last = i == DEPTH - 1
        mod = (cond @ ada_w[i] + ada_b[i]).reshape(-1, 1, 6, D_MODEL)
        mod_c = (cond_ctx @ ada_w[i] + ada_b[i]).reshape(1, 1, 6, D_MODEL)
        hl = modulate(xl, norm_mix[i], mod[:, :, 0], mod[:, :, 1])
        if kind == 2:
            yl = hyena_mixer(hl, hy(j))
            yc = None if last else hyena_mixer(modulate(xc, norm_mix[i], mod_c[:, :, 0], mod_c[:, :, 1]), hy(j))
        else:
            hc = modulate(xc, norm_mix[i], mod_c[:, :, 0], mod_c[:, :, 1])
            if kind == 0:
                yc, yl = rwkv_mixer(hc, hl, not last, rw(j))
            else:
                yc, yl = gla_mixer(hc, hl, not last, gla(j))
        xl = xl + mod[:, :, 2] * yl
        xl = xl + mod[:, :, 5] * peer(modulate(xl, norm_ffn[i], mod[:, :, 3], mod[:, :, 4]), pe(i))
        if not last:
            xc = xc + mod_c[:, :, 2] * yc
            xc = xc + mod_c[:, :, 5] * peer(modulate(xc, norm_ffn[i], mod_c[:, :, 3], mod_c[:, :, 4]), pe(i))
    return _final_norm(xl, norm_out)
```

```python
import math
import functools
import jax, jax.numpy as jnp
from jax import lax
from jax.experimental import pallas as pl
from jax.experimental.pallas import tpu as pltpu

D_MODEL = 1024
BATCH = 8
SEQ = 8192
DEPTH = 4

GRID_W = 64
CTX_LEN = 256
N_MIXERS = 3
NORM_EPS = 1e-6

RW_HEAD = 64
RW_HEADS = D_MODEL // RW_HEAD
RW_GN_EPS = 64e-5

GLA_HEADS = 4
GLA_DK = D_MODEL // 2 // GLA_HEADS
GLA_DV = D_MODEL // GLA_HEADS
GLA_GATE_NORM = 16.0
GLA_CHUNK = 64

HY_ORDER = 2
HY_EMB = 33
HY_BANDS = (HY_EMB - 1) // 2
HY_FAST_DECAY = 0.3
HY_SLOW_DECAY = 1.5
HY_TARGET = 1e-2

PEER_HEADS = 8
PEER_NKEYS = 128
PEER_QDIM = 256
PEER_HALF = PEER_QDIM // 2
PEER_TOPK = 16
PEER_BLOCK = 128


def _final_norm_kernel(x_ref, g_ref, o_ref):
    x = x_ref[...]
    y = x * lax.rsqrt(jnp.mean(x * x, -1, keepdims=True) + NORM_EPS)
    o_ref[...] = y * g_ref[...]


def _final_norm(x, g):
    b_, l, d = x.shape
    rows = b_ * l
    tile = 1024
    out = pl.pallas_call(
        _final_norm_kernel,
        grid=(rows // tile,),
        in_specs=[pl.BlockSpec((tile, d), lambda i: (i, 0)),
                  pl.BlockSpec((1, d), lambda i: (0, 0))],
        out_specs=pl.BlockSpec((tile, d), lambda i: (i, 0)),
        out_shape=jax.ShapeDtypeStruct((rows, d), x.dtype),
        name="final_norm",
    )(x.reshape(rows, d), g.reshape(1, d))
    return out.reshape(b_, l, d)


RW_CHUNK = 64
RW_LANES = 128
RW_PAIRS_PER_STEP = 4
_HI = lax.Precision.HIGHEST


def _dot(x, y, precision=None):
    return jnp.dot(x, y, precision=precision, preferred_element_type=jnp.float32)


def _dot_nt(x, y, precision=None):
    return lax.dot_general(x, y, (((1,), (1,)), ((), ())), precision=precision,
                           preferred_element_type=jnp.float32)


def _dot_tn(x, y, precision=None):
    return lax.dot_general(x, y, (((0,), (0,)), ((), ())), precision=precision,
                           preferred_element_type=jnp.float32)


def _unit_triangular_inverses(mats):
    c = mats[0].shape[0]
    row = lax.broadcasted_iota(jnp.int32, (c, 2 * c), 0)
    col = lax.broadcasted_iota(jnp.int32, (c, 2 * c), 1)
    left = col < c
    xs = [jnp.where(left, jnp.concatenate([a, a], axis=1), jnp.where(col - c == row, 1.0, 0.0)) for a in mats]
    steps = c.bit_length() - 1
    for _ in range(steps):
        xs = [_dot(x[:, :c], x, _HI) + jnp.where(left, 0.0, x) for x in xs]
    return xs


def _rwkv_chunk_pairs(seqs, ms, reverse):
    f32 = jnp.float32
    C = RW_CHUNK
    row = lax.broadcasted_iota(jnp.int32, (C, C), 0)
    col = lax.broadcasted_iota(jnp.int32, (C, C), 1)
    if reverse:
        incl, strict = row <= col, row < col
    else:
        incl, strict = row >= col, row > col
    tri = jnp.where(incl, 1.0, 0.0).astype(f32)
    lane = lax.broadcasted_iota(jnp.int32, (1, RW_LANES), 1)
    head_masks = [jnp.where(lane < RW_HEAD, 1.0, 0.0).astype(f32),
                  jnp.where(lane >= RW_HEAD, 1.0, 0.0).astype(f32)]
    ri = lax.broadcasted_iota(jnp.int32, (RW_LANES, RW_LANES), 0)
    ci = lax.broadcasted_iota(jnp.int32, (RW_LANES, RW_LANES), 1)
    same_head = (ri < RW_HEAD) == (ci < RW_HEAD)
    n = len(seqs)

    cums = [_dot(tri, s[1], _HI) for s in seqs]
    pre = []
    for (r, lw, k, v, kk, bb), cum in zip(seqs, cums):
        mid = cum[C // 2:C // 2 + 1, :]
        tot = cum[0:1, :] if reverse else cum[C - 1:C, :]
        e_inv = jnp.exp(mid - cum)
        e_out = jnp.exp(tot - cum)
        pre.append(dict(
            tot=tot, v=v,
            a_rel=-kk * jnp.exp(cum - lw - mid), b_rel=bb * e_inv, k_rel=k * e_inv,
            r_rel=r * jnp.exp(cum - mid), a_abs=-kk * jnp.exp(cum - lw), r_abs=r * jnp.exp(cum),
            k_out=k * e_out, b_out=bb * e_out))
    uns = [_dot(p["a_abs"], m) for p, m in zip(pre, ms)]
    yns = [_dot(p["r_abs"], m) for p, m in zip(pre, ms)]
    a_hs = [[p["a_rel"] * mh for mh in head_masks] for p in pre]
    a_abs_ = [[jnp.where(strict, _dot_nt(a_h, p["b_rel"]), 0.0) for a_h in a_hs[i]] for i, p in enumerate(pre)]
    a_aks = [[jnp.where(strict, _dot_nt(a_h, p["k_rel"]), 0.0) for a_h in a_hs[i]] for i, p in enumerate(pre)]
    rhss = [[uns[i] + _dot(a_ak, p["v"]) for a_ak in a_aks[i]] for i, p in enumerate(pre)]
    invs = _unit_triangular_inverses([a for pair in a_abs_ for a in pair])
    us = []
    for i in range(n):
        parts = [_dot(invs[2 * i + h][:, C:], rhss[i][h], _HI) for h in range(2)]
        us.append(head_masks[0] * parts[0] + head_masks[1] * parts[1])
    r_hs = [[p["r_rel"] * mh for mh in head_masks] for p in pre]
    r_bs = [[jnp.where(incl, _dot_nt(r_h, p["b_rel"]), 0.0) for r_h in r_hs[i]] for i, p in enumerate(pre)]
    r_ks = [[jnp.where(incl, _dot_nt(r_h, p["k_rel"]), 0.0) for r_h in r_hs[i]] for i, p in enumerate(pre)]
    out = []
    for i, p in enumerate(pre):
        y = yns[i]
        for h, mh in enumerate(head_masks):
            y = y + mh * (_dot(r_bs[i][h], us[i]) + _dot(r_ks[i][h], p["v"]))
        decay = jnp.transpose(jnp.broadcast_to(jnp.exp(p["tot"]), (RW_LANES, RW_LANES)))
        m_new = decay * ms[i] + jnp.where(
            same_head, _dot_tn(p["b_out"], us[i]) + _dot_tn(p["k_out"], p["v"]), 0.0)
        out.append((y, m_new))
    return out


def _rwkv_chunk_kernel(r_ref, lw_ref, k_ref, v_ref, kk_ref, bb_ref, m0_ref,
                       y_ref, mfin_ref, m_scr, *, reverse):
    c_idx = pl.program_id(2)
    n_chunks = pl.num_programs(2)

    @pl.when(c_idx == 0)
    def _():
        m_scr[...] = m0_ref[0]

    lanes = [slice(p * RW_LANES, (p + 1) * RW_LANES) for p in range(RW_PAIRS_PER_STEP)]
    seqs = [tuple(ref[0, :, sl] for ref in (r_ref, lw_ref, k_ref, v_ref, kk_ref, bb_ref)) for sl in lanes]
    results = _rwkv_chunk_pairs(seqs, [m_scr[p] for p in range(RW_PAIRS_PER_STEP)], reverse)
    for p, (sl, (y, m_new)) in enumerate(zip(lanes, results)):
        y_ref[0, :, sl] = y
        m_scr[p] = m_new

    @pl.when(c_idx == n_chunks - 1)
    def _():
        mfin_ref[0] = m_scr[...]


def rwkv_chunked(r, lw, k, v, kk, bb, s0, reverse, interpret=False):
    b_, l, d = r.shape
    n_pairs = d // RW_LANES
    n_chunks = l // RW_CHUNK
    hn = RW_HEAD
    st = jnp.swapaxes(s0, -1, -2).reshape(b_, n_pairs, 2, hn, hn)
    zero = jnp.zeros_like(st[:, :, 0])
    m0 = jnp.concatenate([jnp.concatenate([st[:, :, 0], zero], -1),
                          jnp.concatenate([zero, st[:, :, 1]], -1)], -2)
    if reverse:
        seq_map = lambda b, p, c: (b, n_chunks - 1 - c, p)
    else:
        seq_map = lambda b, p, c: (b, c, p)
    pps = RW_PAIRS_PER_STEP
    seq_spec = pl.BlockSpec((1, RW_CHUNK, pps * RW_LANES), seq_map)
    st_spec = pl.BlockSpec((1, pps, RW_LANES, RW_LANES), lambda b, p, c: (b, p, 0, 0))
    y, mfin = pl.pallas_call(
        functools.partial(_rwkv_chunk_kernel, reverse=reverse),
        grid=(b_, n_pairs // pps, n_chunks),
        in_specs=[seq_spec] * 6 + [st_spec],
        out_specs=[seq_spec, st_spec],
        out_shape=[jax.ShapeDtypeStruct((b_, l, d), jnp.float32),
                   jax.ShapeDtypeStruct((b_, n_pairs, RW_LANES, RW_LANES), jnp.float32)],
        scratch_shapes=[pltpu.VMEM((pps, RW_LANES, RW_LANES), jnp.float32)],
        compiler_params=pltpu.CompilerParams(
            dimension_semantics=("parallel", "parallel", "arbitrary")),
        name="rwkv_scan_bwd" if reverse else "rwkv_scan_fwd",
        interpret=interpret,
    )(r, lw, k, v, kk, bb, m0)
    s_fin = jnp.stack([mfin[:, :, :hn, :hn], mfin[:, :, hn:, hn:]], 2)
    return y, jnp.swapaxes(s_fin.reshape(b_, 2 * n_pairs, hn, hn), -1, -2)


PEER_TOK = 16
PEER_SEL = PEER_HEADS * PEER_TOPK
PEER_ROWS = PEER_TOK * PEER_SEL
SUBLANES = 8
LANES = 128
PEER_GROUPS = PEER_ROWS // SUBLANES
PEER_DTILES = D_MODEL // LANES
PEER_RB = 128


def _peer_issue_group(idx_ref, uv_hbm, buf, sem, slot, grp):
    for sub in range(SUBLANES):
        e = idx_ref[0, 0, grp * SUBLANES + sub]
        pltpu.make_async_copy(uv_hbm.at[e], buf.at[slot, grp, :, sub, :],
                              sem.at[slot]).start(priority=sub % 2)


def _peer_wait_slot(buf, sem, slot):
    pltpu.make_async_copy(buf.at[1 - slot], buf.at[slot], sem.at[slot]).wait()


def _peer_expert_kernel(idx_first_ref, idx_next_ref, h_ref, g_ref, x_ref, og_ref, uv_hbm, o_ref, buf, sem):
    i = pl.program_id(0)
    n = pl.num_programs(0)

    @pl.when(i == 0)
    def _():
        def body(grp, carry):
            _peer_issue_group(idx_first_ref, uv_hbm, buf, sem, 0, grp)
            return carry
        lax.fori_loop(0, PEER_GROUPS, body, 0)

    bf16 = jnp.bfloat16
    n_rb = PEER_ROWS // PEER_RB
    grp_rb = PEER_RB // SUBLANES
    pieces = 2 * PEER_DTILES * n_rb
    per_piece = PEER_GROUPS // pieces

    def step(slot):
        _peer_wait_slot(buf, sem, slot)

        def issue_piece(piece):
            for grp in range(piece * per_piece, (piece + 1) * per_piece):
                _peer_issue_group(idx_next_ref, uv_hbm, buf, sem, 1 - slot, grp)

        def half(rb, c, low):
            x = buf[slot, rb * grp_rb:(rb + 1) * grp_rb, c, :, :].reshape(PEER_RB, LANES)
            x = (x << 16) if low else (x & jnp.uint32(0xFFFF0000))
            return lax.bitcast_convert_type(x, jnp.float32).astype(bf16)

        h = h_ref[...].astype(bf16)
        piece = 0
        blocks = []
        for rb in range(n_rb):
            sc = jnp.zeros((PEER_TOK, PEER_RB), jnp.float32)
            for c in range(PEER_DTILES):
                sc = sc + _dot_nt(h[:, c * LANES:(c + 1) * LANES], half(rb, c, False))
                issue_piece(piece)
                piece += 1
            blocks.append(sc)
        scores = jnp.concatenate(blocks, axis=1)
        tok = lax.broadcasted_iota(jnp.int32, (PEER_TOK, PEER_ROWS), 0)
        owner = lax.broadcasted_iota(jnp.int32, (PEER_TOK, PEER_ROWS), 1) // PEER_SEL
        own = tok == owner
        s_row = jnp.sum(jnp.where(own, scores, 0.0), axis=0, keepdims=True)
        act = 0.5 * s_row * (1.0 + lax.erf(s_row * (2.0 ** -0.5))) * g_ref[0]
        a_mat = jnp.where(own, jnp.broadcast_to(act, (PEER_TOK, PEER_ROWS)), 0.0).astype(bf16)
        for c in range(PEER_DTILES):
            acc = jnp.zeros((PEER_TOK, LANES), jnp.float32)
            for rb in range(n_rb):
                acc = acc + _dot(a_mat[:, rb * PEER_RB:(rb + 1) * PEER_RB], half(rb, c, True))
                issue_piece(piece)
                piece += 1
            sl = slice(c * LANES, (c + 1) * LANES)
            o_ref[:, sl] = x_ref[:, sl] + og_ref[0, :, sl] * acc

        @pl.when(i == n - 1)
        def _():
            _peer_wait_slot(buf, sem, 1 - slot)

    step(i % 2)


def pack_expert_table(u, v):
    bits = lambda t: lax.bitcast_convert_type(t.astype(jnp.bfloat16), jnp.uint16).astype(jnp.uint32)
    return ((bits(u) << 16) | bits(v)).reshape(u.shape[0], PEER_DTILES, LANES)


def peer_experts(h, e, g, x, out_gate, uv, rows_per_batch, interpret=False):
    n, d = h.shape
    steps = n // PEER_TOK
    idx = e.reshape(steps, 1, PEER_ROWS)
    gate = g.reshape(steps, 1, PEER_ROWS)
    return pl.pallas_call(
        _peer_expert_kernel,
        grid=(steps,),
        in_specs=[
            pl.BlockSpec((1, 1, PEER_ROWS), lambda i: (0, 0, 0), memory_space=pltpu.SMEM),
            pl.BlockSpec((1, 1, PEER_ROWS), lambda i: (jnp.minimum(i + 1, steps - 1), 0, 0),
                         memory_space=pltpu.SMEM),
            pl.BlockSpec((PEER_TOK, d), lambda i: (i, 0)),
            pl.BlockSpec((1, 1, PEER_ROWS), lambda i: (i, 0, 0)),
            pl.BlockSpec((PEER_TOK, d), lambda i: (i, 0)),
            _per_batch_spec(out_gate, rows_per_batch, PEER_TOK),
            pl.BlockSpec(memory_space=pl.ANY),
        ],
        out_specs=pl.BlockSpec((PEER_TOK, d), lambda i: (i, 0)),
        out_shape=jax.ShapeDtypeStruct((n, d), jnp.float32),
        scratch_shapes=[pltpu.VMEM((2, PEER_GROUPS, PEER_DTILES, SUBLANES, LANES), jnp.uint32),
                        pltpu.SemaphoreType.DMA((2,))],
        compiler_params=pltpu.CompilerParams(
            dimension_semantics=("arbitrary",),
            vmem_limit_bytes=40 * 1024 * 1024),
        name="peer_experts",
        interpret=interpret,
    )(idx, idx, h, gate, x, out_gate, uv)


ROUTE_TOK = 128


def _top_rounds(s, rounds):
    n_rows, t = s.shape
    row = lax.broadcasted_iota(jnp.int32, (n_rows, t), 0)
    out_row = lax.broadcasted_iota(jnp.int32, (rounds, t), 0)
    vals = jnp.zeros((rounds, t), jnp.float32)
    idxs = jnp.zeros((rounds, t), jnp.int32)
    for r in range(rounds):
        m = jnp.max(s, axis=0, keepdims=True)
        i = jnp.min(jnp.where(s == m, row, n_rows), axis=0, keepdims=True)
        vals = jnp.where(out_row == r, m, vals)
        idxs = jnp.where(out_row == r, i, idxs)
        s = jnp.where(row == i, -jnp.inf, s)
    return vals, idxs


def _top_pair_sums(x, y):
    k, t = x.shape
    assert k == 16
    sub = lax.broadcasted_iota(jnp.int32, (SUBLANES, t), 0)
    blocks = [x[0:1, :] + y]
    for a in range(1, 8):
        blocks.append(jnp.where(sub < k // (a + 1), x[a:a + 1, :] + y[0:SUBLANES, :], -jnp.inf))
    blocks.append(x[8:16, :] + y[0:1, :])
    vals, r = _top_rounds(jnp.concatenate(blocks, axis=0), k)
    blk = r // SUBLANES
    a_idx = jnp.where(r < k, 0, jnp.where(blk < 9, blk - 1, r - 64))
    b_idx = jnp.where(r < k, r, jnp.where(blk < 9, r % SUBLANES, 0))
    return vals, a_idx, b_idx


def _select_rows(table, sel):
    k, t = table.shape
    out = jnp.zeros(sel.shape, table.dtype)
    for j in range(k):
        out = jnp.where(sel == j, table[j:j + 1, :], out)
    return out


def _peer_route_kernel(x_ref, nw_ref, shift_ref, scale_ref, wq_ref, keys_ref, h_ref, e_ref, g_ref):
    x = x_ref[...]
    h = x * lax.rsqrt(jnp.mean(x * x, -1, keepdims=True) + NORM_EPS) * nw_ref[...]
    h = h * (1 + scale_ref[0]) + shift_ref[0]
    h_ref[...] = h
    q = _dot(h, wq_ref[...])
    e_rows, g_rows = [], []
    for hd in range(PEER_HEADS):
        sv, si = [], []
        for p in range(2):
            c0 = (hd * 2 + p) * PEER_HALF
            s_t = _dot_nt(keys_ref[hd, p], q[:, c0:c0 + PEER_HALF])
            v_, i_ = _top_rounds(s_t, PEER_TOPK)
            sv.append(v_)
            si.append(i_)
        fv, fa, fb = _top_pair_sums(sv[0], sv[1])
        i1 = _select_rows(si[0], fa)
        i2 = _select_rows(si[1], fb)
        e_rows.append(i1 * PEER_NKEYS + i2)
        ex = jnp.exp(fv - jnp.max(fv, axis=0, keepdims=True))
        g_rows.append(ex / jnp.sum(ex, axis=0, keepdims=True))
    e_ref[...] = jnp.concatenate(e_rows, axis=0).T
    g_ref[...] = jnp.concatenate(g_rows, axis=0).T


def _per_batch_spec(vec, rows_per_batch, tile):
    if vec.shape[0] == 1:
        return pl.BlockSpec((1, 1, vec.shape[2]), lambda i: (0, 0, 0))
    return pl.BlockSpec((1, 1, vec.shape[2]), lambda i: (i // (rows_per_batch // tile), 0, 0))


def peer_route(x, norm_w, shift, scale, wq, keys, rows_per_batch, interpret=False):
    n, d = x.shape
    return pl.pallas_call(
        _peer_route_kernel,
        grid=(n // ROUTE_TOK,),
        in_specs=[pl.BlockSpec((ROUTE_TOK, d), lambda i: (i, 0)),
                  pl.BlockSpec((1, d), lambda i: (0, 0)),
                  _per_batch_spec(shift, rows_per_batch, ROUTE_TOK),
                  _per_batch_spec(scale, rows_per_batch, ROUTE_TOK),
                  pl.BlockSpec(wq.shape, lambda i: (0, 0)),
                  pl.BlockSpec(keys.shape, lambda i: (0, 0, 0, 0))],
        out_specs=[pl.BlockSpec((ROUTE_TOK, d), lambda i: (i, 0)),
                   pl.BlockSpec((ROUTE_TOK, PEER_SEL), lambda i: (i, 0)),
                   pl.BlockSpec((ROUTE_TOK, PEER_SEL), lambda i: (i, 0))],
        out_shape=[jax.ShapeDtypeStruct((n, d), jnp.float32),
                   jax.ShapeDtypeStruct((n, PEER_SEL), jnp.int32),
                   jax.ShapeDtypeStruct((n, PEER_SEL), jnp.float32)],
        compiler_params=pltpu.CompilerParams(
            dimension_semantics=("parallel",), vmem_limit_bytes=40 * 1024 * 1024),
        name="peer_route",
        interpret=interpret,
    )(x, norm_w, shift, scale, wq, keys)


def rmsnorm(x, g):
    xf = x.astype(jnp.float32)
    y = xf * lax.rsqrt(jnp.mean(xf * xf, -1, keepdims=True) + NORM_EPS)
    return y.astype(x.dtype) * g


def modulate(x, g, shift, scale):
    return rmsnorm(x, g) * (1 + scale) + shift


def shift_grid(h):
    b_, l, d = h.shape
    rows = l // GRID_W
    g = h.reshape(b_, rows, GRID_W, d)
    q = d // 4
    left = jnp.pad(g[:, :, :-1, :q], ((0, 0), (0, 0), (1, 0), (0, 0)))
    right = jnp.pad(g[:, :, 1:, q:2 * q], ((0, 0), (0, 0), (0, 1), (0, 0)))
    up = jnp.pad(g[:, :-1, :, 2 * q:3 * q], ((0, 0), (1, 0), (0, 0), (0, 0)))
    down = jnp.pad(g[:, 1:, :, 3 * q:], ((0, 0), (0, 1), (0, 0), (0, 0)))
    return jnp.concatenate([left, right, up, down], -1).reshape(b_, l, d)


def shift_seq(h):
    d = h.shape[-1] // 2
    prev = jnp.pad(h[:, :-1, :d], ((0, 0), (1, 0), (0, 0)))
    nxt = jnp.pad(h[:, 1:, d:], ((0, 0), (0, 1), (0, 0)))
    return jnp.concatenate([prev, nxt], -1)


def rwkv_mixer(hc, hl, need_ctx, p):
    (mu, w_rkv, w0, w1, w2, a0, a1, a2, g1, g2, k_k, k_a, r_k, ln_w, ln_b, w_o) = p
    f32 = jnp.float32

    def run(h, shifted, s_init):
        b_, l, d = h.shape
        heads = lambda t: t.reshape(b_, l, RW_HEADS, RW_HEAD).astype(f32)
        xx = shifted - h
        rkv = jnp.einsum('jbld,jde->jble', h[None] + xx[None] * mu[:3, None, None, :], w_rkv)
        r, k, v = rkv[0].astype(f32), rkv[1], rkv[2].astype(f32)
        xw, xa, xg = h + xx * mu[3], h + xx * mu[4], h + xx * mu[5]
        kk = heads(k * k_k)
        kk = (kk * lax.rsqrt(jnp.sum(kk * kk, -1, keepdims=True) + 1e-12)).reshape(b_, l, d)
        y = jnp.zeros_like(v)
        k_sum = jnp.zeros_like(v)
        finals = []
        for dr in range(2):
            wlog = -jax.nn.softplus(-(w0[dr] + jnp.tanh(xw @ w1[dr]) @ w2[dr]).astype(f32)) - 0.5
            a = jax.nn.sigmoid((a0[dr] + (xa @ a1[dr]) @ a2[dr]).astype(f32))
            kd = (k * (1 + (a - 1) * k_a)).astype(f32)
            ys, sf = rwkv_chunked(r, -jnp.exp(wlog), kd, v, kk, kk * a, s_init[dr], dr == 1)
            y = y + ys
            k_sum = k_sum + kd
            finals.append(sf)
        return (heads(y), heads(r), heads(v), heads(k_sum), xg), finals

    def post(h, y, r, v, k_sum, xg):
        b_, l, d = h.shape
        mean = jnp.mean(y, -1, keepdims=True)
        var = jnp.mean(jnp.square(y - mean), -1, keepdims=True)
        yn = ((y - mean) * lax.rsqrt(var + RW_GN_EPS)).reshape(b_, l, d) * ln_w + ln_b
        bonus = jnp.sum(r * k_sum * r_k.reshape(RW_HEADS, RW_HEAD).astype(f32), -1, keepdims=True) * v
        g = jax.nn.sigmoid(xg @ g1) @ g2
        return ((yn + bonus.reshape(b_, l, d)).astype(h.dtype) * g) @ w_o

    s_zero = jnp.zeros((hc.shape[0], RW_HEADS, RW_HEAD, RW_HEAD), f32)
    out_c, s_ctx = run(hc, shift_seq(hc), (s_zero, s_zero))
    out_l, _ = run(hl, shift_grid(hl), s_ctx)
    yc = post(hc, *out_c) if need_ctx else None
    return yc, post(hl, *out_l)


def _gla_chunk_kernel(qf_ref, kf_ref, vf_ref, ldf_ref, qb_ref, kb_ref, vb_ref, ldb_ref, s0f_ref, s0b_ref,
                      of_ref, ob_ref, sff_ref, sfb_ref, sf_scr, sb_scr):
    f32 = jnp.float32
    C = GLA_CHUNK
    c_idx = pl.program_id(2)
    n_chunks = pl.num_programs(2)

    @pl.when(c_idx == 0)
    def _():
        sf_scr[...] = s0f_ref[0, 0]
        sb_scr[...] = s0b_ref[0, 0]

    row = lax.broadcasted_iota(jnp.int32, (C, C), 0)
    col = lax.broadcasted_iota(jnp.int32, (C, C), 1)
    incls = [row >= col, row <= col]
    qs = [qf_ref[0], qb_ref[0]]
    ks = [kf_ref[0], kb_ref[0]]
    vs = [vf_ref[0], vb_ref[0]]
    lds = [ldf_ref[0], ldb_ref[0]]
    states = [sf_scr[...], sb_scr[...]]
    cums = [_dot(jnp.where(m, 1.0, 0.0).astype(f32), ld, _HI) for m, ld in zip(incls, lds)]
    mids = [cum[C // 2:C // 2 + 1, :] for cum in cums]
    lasts = [cums[0][C - 1:C, :], cums[1][0:1, :]]
    scores = [jnp.where(m, _dot_nt(q * jnp.exp(cum - mid), k * jnp.exp(mid - cum)), 0.0)
              for m, q, k, cum, mid in zip(incls, qs, ks, cums, mids)]
    outs = [_dot(sc, v) + _dot(q * jnp.exp(cum), s)
            for sc, v, q, cum, s in zip(scores, vs, qs, cums, states)]
    decays = [jnp.transpose(jnp.broadcast_to(jnp.exp(last), (GLA_DK, GLA_DK))) for last in lasts]
    new_states = [jnp.concatenate([dec] * (GLA_DV // GLA_DK), axis=1) * s + _dot_tn(k * jnp.exp(last - cum), v)
                  for dec, s, k, last, cum, v in zip(decays, states, ks, lasts, cums, vs)]
    of_ref[0] = outs[0]
    ob_ref[0] = outs[1]
    sf_scr[...] = new_states[0]
    sb_scr[...] = new_states[1]

    @pl.when(c_idx == n_chunks - 1)
    def _():
        sff_ref[0, 0] = new_states[0]
        sfb_ref[0, 0] = new_states[1]


def gla_scan(q, k, v, ld_f, ld_b, s0_f, s0_b, interpret=False):
    b_, l, _ = q.shape
    n = l // GLA_CHUNK
    fwd = lambda b, h, c: (b, c, h)
    bwd = lambda b, h, c: (b, n - 1 - c, h)
    kspec = lambda m: pl.BlockSpec((1, GLA_CHUNK, GLA_DK), m)
    vspec = lambda m: pl.BlockSpec((1, GLA_CHUNK, GLA_DV), m)
    sspec = pl.BlockSpec((1, 1, GLA_DK, GLA_DV), lambda b, h, c: (b, h, 0, 0))
    o_f, o_b, sf, sb = pl.pallas_call(
        _gla_chunk_kernel,
        grid=(b_, GLA_HEADS, n),
        in_specs=[kspec(fwd), kspec(fwd), vspec(fwd), kspec(fwd),
                  kspec(bwd), kspec(bwd), vspec(bwd), kspec(bwd), sspec, sspec],
        out_specs=[vspec(fwd), vspec(bwd), sspec, sspec],
        out_shape=[jax.ShapeDtypeStruct(v.shape, jnp.float32), jax.ShapeDtypeStruct(v.shape, jnp.float32),
                   jax.ShapeDtypeStruct(s0_f.shape, jnp.float32), jax.ShapeDtypeStruct(s0_b.shape, jnp.float32)],
        scratch_shapes=[pltpu.VMEM((GLA_DK, GLA_DV), jnp.float32), pltpu.VMEM((GLA_DK, GLA_DV), jnp.float32)],
        compiler_params=pltpu.CompilerParams(
            dimension_semantics=("parallel", "parallel", "arbitrary")),
        name="gla_scan",
        interpret=interpret,
    )(q, k, v, ld_f, q, k, v, ld_b, s0_f, s0_b)
    return o_f + o_b, sf, sb


def gla_mixer(hc, hl, need_ctx, p):
    w_in, wg1, wg2, bg, norm_w, w_o = p
    f32 = jnp.float32
    hk = GLA_HEADS * GLA_DK

    def run(h, s_init):
        b_, l, d = h.shape
        q, k, v, og = jnp.split(h @ w_in, [hk, 2 * hk, 2 * hk + d], -1)
        q = q.astype(f32) * GLA_DK ** -0.5
        logd = [jax.nn.log_sigmoid(((h @ wg1[dr]) @ wg2[dr] + bg[dr]).astype(f32)) / GLA_GATE_NORM
                for dr in range(2)]
        o, s_f, s_b = gla_scan(q, k.astype(f32), v.astype(f32), logd[0], logd[1], s_init[0], s_init[1])
        return o.reshape(b_, l, GLA_HEADS, GLA_DV), og, (s_f, s_b)

    def post(h, o, og):
        b_, l, d = h.shape
        on = o * lax.rsqrt(jnp.mean(o * o, -1, keepdims=True) + NORM_EPS) * norm_w
        return (on.reshape(b_, l, d).astype(h.dtype) * jax.nn.silu(og)) @ w_o

    s_zero = jnp.zeros((hc.shape[0], GLA_HEADS, GLA_DK, GLA_DV), f32)
    oc, ogc, s_ctx = run(hc, (s_zero, s_zero))
    ol, ogl, _ = run(hl, s_ctx)
    yc = post(hc, oc, ogc) if need_ctx else None
    return yc, post(hl, ol, ogl)


def conv3(z, w, b):
    zp = jnp.pad(z, ((0, 0), (1, 1), (0, 0)))
    return zp[:, :-2] * w[0] + zp[:, 1:-1] * w[1] + zp[:, 2:] * w[2] + b


def hyena_filters(l, f1, fb1, f2, fb2, f3, fb3, f4, freq):
    f32 = jnp.float32
    t = jnp.linspace(0.0, 1.0, l, dtype=f32)[:, None]
    w = 2 * math.pi * jnp.arange(l, dtype=f32)[:, None] / l
    f = jnp.linspace(1e-4, HY_BANDS - 1, HY_BANDS, dtype=f32)[None]
    z = jnp.concatenate([t, jnp.cos(f * w), -jnp.sin(f * w)], -1)
    fr = freq.astype(f32)
    hdn = jnp.sin(fr * (z @ f1.astype(f32) + fb1.astype(f32)))
    hdn = jnp.sin(fr * (hdn @ f2.astype(f32) + fb2.astype(f32)))
    hdn = jnp.sin(fr * (hdn @ f3.astype(f32) + fb3.astype(f32)))
    filt = (hdn @ f4.astype(f32)).reshape(l, HY_ORDER, 2, D_MODEL)
    min_decay = math.log(HY_TARGET) / HY_SLOW_DECAY
    max_decay = math.log(HY_TARGET) / HY_FAST_DECAY
    deltas = jnp.linspace(min_decay, max_decay, D_MODEL, dtype=f32)
    filt = filt * jnp.exp(-t * jnp.abs(deltas))[:, None, None, :]
    fwd, bwd = filt[:, :, 0], filt[:, :, 1]
    kc = jnp.concatenate([fwd, jnp.zeros_like(fwd[:1]), jnp.flip(bwd[1:], 0)], 0)
    return kc / jnp.sum(jnp.abs(kc), 0, keepdims=True)


def long_conv(u, kc_o, bias_o):
    b_, l, _ = u.shape
    half = b_ // 2
    z = lax.complex(u[:half], u[half:])
    kf = jnp.fft.fft(kc_o.astype(jnp.complex64), n=2 * l, axis=0)
    out = jnp.fft.ifft(jnp.fft.fft(z, n=2 * l, axis=1) * kf[None], n=2 * l, axis=1)[:, :l]
    return jnp.concatenate([jnp.real(out), jnp.imag(out)], axis=0) + u * bias_o


def hyena_mixer(h, p):
    (w_in, b_in, sc_w, sc_b, f1, fb1, f2, fb2, f3, fb3, f4, freq, bias, w_o, b_o) = p
    l = h.shape[1]
    z = conv3(h @ w_in + b_in, sc_w, sc_b).astype(jnp.float32)
    v, x1, x2 = jnp.split(z, 3, -1)
    kc = hyena_filters(l, f1, fb1, f2, fb2, f3, fb3, f4, freq)
    bias = bias.astype(jnp.float32)
    y = x1 * long_conv(v, kc[:, 0], bias[0])
    y = x2 * long_conv(y, kc[:, 1], bias[1])
    return y.astype(h.dtype) @ w_o + b_o


def peer_layer(x, norm_w, shift, scale, out_gate, p):
    wq, keys, uv = p
    b_, l, d = x.shape
    xf = x.reshape(b_ * l, d)
    h, e, g = peer_route(xf, norm_w.reshape(1, d), shift, scale, wq, keys, l)
    return peer_experts(h, e, g, xf, out_gate, uv, l).reshape(b_, l, d)


def kernel(x, c, ctx, c_ctx, ada_w, ada_b, norm_mix, norm_ffn, norm_out,
           rw_mu, rw_wrkv, rw_w0, rw_w1, rw_w2, rw_a0, rw_a1, rw_a2, rw_g1, rw_g2,
           rw_kk, rw_ka, rw_rk, rw_lnw, rw_lnb, rw_wo,
           gla_win, gla_wg1, gla_wg2, gla_bg, gla_normw, gla_wo,
           hy_win, hy_bin, hy_scw, hy_scb, hy_f1, hy_fb1, hy_f2, hy_fb2, hy_f3, hy_fb3,
           hy_f4, hy_freq, hy_bias, hy_wo, hy_bo,
           pe_wq, pe_keys, pe_u, pe_v):
    rw = lambda j: (rw_mu[j], rw_wrkv[j], rw_w0[j], rw_w1[j], rw_w2[j], rw_a0[j], rw_a1[j], rw_a2[j],
                    rw_g1[j], rw_g2[j], rw_kk[j], rw_ka[j], rw_rk[j], rw_lnw[j], rw_lnb[j], rw_wo[j])
    gla = lambda j: (gla_win[j], gla_wg1[j], gla_wg2[j], gla_bg[j], gla_normw[j], gla_wo[j])
    hy = lambda j: (hy_win[j], hy_bin[j], hy_scw[j], hy_scb[j], hy_f1[j], hy_fb1[j], hy_f2[j], hy_fb2[j],
                    hy_f3[j], hy_fb3[j], hy_f4[j], hy_freq[j], hy_bias[j], hy_wo[j], hy_bo[j])
    pe_uv = [pack_expert_table(pe_u[i], pe_v[i]) for i in range(DEPTH)]
    pe = lambda i: (pe_wq[i], pe_keys[i], pe_uv[i])

    xl, xc = x, ctx
    cond = jax.nn.silu(c)
    cond_ctx = jax.nn.silu(c_ctx)
    for i in range(DEPTH):
        kind, j = i % N_MIXERS, i // N_MIXERS
        last = i == DEPTH - 1
        mod = (cond @ ada_w[i] + ada_b[i]).reshape(-1, 1, 6, D_MODEL)
        mod_c = (cond_ctx @ ada_w[i] + ada_b[i]).reshape(1, 1, 6, D_MODEL)
        hl = modulate(xl, norm_mix[i], mod[:, :, 0], mod[:, :, 1])
        if kind == 2:
            yl = hyena_mixer(hl, hy(j))
            yc = None if last else hyena_mixer(modulate(xc, norm_mix[i], mod_c[:, :, 0], mod_c[:, :, 1]), hy(j))
        else:
            hc = modulate(xc, norm_mix[i], mod_c[:, :, 0], mod_c[:, :, 1])
            if kind == 0:
                yc, yl = rwkv_mixer(hc, hl, not last, rw(j))
            else:
                yc, yl = gla_mixer(hc, hl, not last, gla(j))
        xl = xl + mod[:, :, 2] * yl
        xl = peer_layer(xl, norm_ffn[i], mod[:, :, 3], mod[:, :, 4], mod[:, :, 5], pe(i))
        if not last:
            xc = xc + mod_c[:, :, 2] * yc
            xc = peer_layer(xc, norm_ffn[i], mod_c[:, :, 3], mod_c[:, :, 4], mod_c[:, :, 5], pe(i))
    return _final_norm(xl, norm_out)
```

```python
import math
import functools
import jax, jax.numpy as jnp
from jax import lax
from jax.experimental import pallas as pl
from jax.experimental.pallas import tpu as pltpu

D_MODEL = 1024
BATCH = 8
SEQ = 8192
DEPTH = 4

GRID_W = 64
CTX_LEN = 256
N_MIXERS = 3
NORM_EPS = 1e-6

RW_HEAD = 64
RW_HEADS = D_MODEL // RW_HEAD
RW_GN_EPS = 64e-5

GLA_HEADS = 4
GLA_DK = D_MODEL // 2 // GLA_HEADS
GLA_DV = D_MODEL // GLA_HEADS
GLA_GATE_NORM = 16.0
GLA_CHUNK = 64

HY_ORDER = 2
HY_EMB = 33
HY_BANDS = (HY_EMB - 1) // 2
HY_FAST_DECAY = 0.3
HY_SLOW_DECAY = 1.5
HY_TARGET = 1e-2

PEER_HEADS = 8
PEER_NKEYS = 128
PEER_QDIM = 256
PEER_HALF = PEER_QDIM // 2
PEER_TOPK = 16
PEER_BLOCK = 128


def _final_norm_kernel(x_ref, g_ref, o_ref):
    x = x_ref[...]
    y = x * lax.rsqrt(jnp.mean(x * x, -1, keepdims=True) + NORM_EPS)
    o_ref[...] = y * g_ref[...]


def _final_norm(x, g):
    b_, l, d = x.shape
    rows = b_ * l
    tile = 1024
    out = pl.pallas_call(
        _final_norm_kernel,
        grid=(rows // tile,),
        in_specs=[pl.BlockSpec((tile, d), lambda i: (i, 0)),
                  pl.BlockSpec((1, d), lambda i: (0, 0))],
        out_specs=pl.BlockSpec((tile, d), lambda i: (i, 0)),
        out_shape=jax.ShapeDtypeStruct((rows, d), x.dtype),
        name="final_norm",
    )(x.reshape(rows, d), g.reshape(1, d))
    return out.reshape(b_, l, d)


RW_CHUNK = 64
RW_LANES = 128
RW_PAIRS_PER_STEP = 4
_HI = lax.Precision.HIGHEST


def _dot(x, y, precision=None):
    return jnp.dot(x, y, precision=precision, preferred_element_type=jnp.float32)


def _dot_nt(x, y, precision=None):
    return lax.dot_general(x, y, (((1,), (1,)), ((), ())), precision=precision,
                           preferred_element_type=jnp.float32)


def _dot_tn(x, y, precision=None):
    return lax.dot_general(x, y, (((0,), (0,)), ((), ())), precision=precision,
                           preferred_element_type=jnp.float32)


def _unit_triangular_inverses(mats):
    c = mats[0].shape[0]
    row = lax.broadcasted_iota(jnp.int32, (c, 2 * c), 0)
    col = lax.broadcasted_iota(jnp.int32, (c, 2 * c), 1)
    left = col < c
    xs = [jnp.where(left, jnp.concatenate([a, a], axis=1), jnp.where(col - c == row, 1.0, 0.0)) for a in mats]
    steps = c.bit_length() - 1
    for _ in range(steps):
        xs = [_dot(x[:, :c], x, _HI) + jnp.where(left, 0.0, x) for x in xs]
    return xs


def _rwkv_chunk_pairs(seqs, ms, reverse):
    f32 = jnp.float32
    C = RW_CHUNK
    row = lax.broadcasted_iota(jnp.int32, (C, C), 0)
    col = lax.broadcasted_iota(jnp.int32, (C, C), 1)
    if reverse:
        incl, strict = row <= col, row < col
    else:
        incl, strict = row >= col, row > col
    tri = jnp.where(incl, 1.0, 0.0).astype(f32)
    lane = lax.broadcasted_iota(jnp.int32, (1, RW_LANES), 1)
    head_masks = [jnp.where(lane < RW_HEAD, 1.0, 0.0).astype(f32),
                  jnp.where(lane >= RW_HEAD, 1.0, 0.0).astype(f32)]
    ri = lax.broadcasted_iota(jnp.int32, (RW_LANES, RW_LANES), 0)
    ci = lax.broadcasted_iota(jnp.int32, (RW_LANES, RW_LANES), 1)
    same_head = (ri < RW_HEAD) == (ci < RW_HEAD)
    n = len(seqs)

    cums = [_dot(tri, s[1], _HI) for s in seqs]
    pre = []
    for (r, lw, k, v, kk, bb), cum in zip(seqs, cums):
        mid = cum[C // 2:C // 2 + 1, :]
        tot = cum[0:1, :] if reverse else cum[C - 1:C, :]
        e_inv = jnp.exp(mid - cum)
        e_out = jnp.exp(tot - cum)
        pre.append(dict(
            tot=tot, v=v,
            a_rel=-kk * jnp.exp(cum - lw - mid), b_rel=bb * e_inv, k_rel=k * e_inv,
            r_rel=r * jnp.exp(cum - mid), a_abs=-kk * jnp.exp(cum - lw), r_abs=r * jnp.exp(cum),
            k_out=k * e_out, b_out=bb * e_out))
    uns = [_dot(p["a_abs"], m) for p, m in zip(pre, ms)]
    yns = [_dot(p["r_abs"], m) for p, m in zip(pre, ms)]
    a_hs = [[p["a_rel"] * mh for mh in head_masks] for p in pre]
    a_abs_ = [[jnp.where(strict, _dot_nt(a_h, p["b_rel"]), 0.0) for a_h in a_hs[i]] for i, p in enumerate(pre)]
    a_aks = [[jnp.where(strict, _dot_nt(a_h, p["k_rel"]), 0.0) for a_h in a_hs[i]] for i, p in enumerate(pre)]
    rhss = [[uns[i] + _dot(a_ak, p["v"]) for a_ak in a_aks[i]] for i, p in enumerate(pre)]
    invs = _unit_triangular_inverses([a for pair in a_abs_ for a in pair])
    us = []
    for i in range(n):
        parts = [_dot(invs[2 * i + h][:, C:], rhss[i][h], _HI) for h in range(2)]
        us.append(head_masks[0] * parts[0] + head_masks[1] * parts[1])
    r_hs = [[p["r_rel"] * mh for mh in head_masks] for p in pre]
    r_bs = [[jnp.where(incl, _dot_nt(r_h, p["b_rel"]), 0.0) for r_h in r_hs[i]] for i, p in enumerate(pre)]
    r_ks = [[jnp.where(incl, _dot_nt(r_h, p["k_rel"]), 0.0) for r_h in r_hs[i]] for i, p in enumerate(pre)]
    out = []
    for i, p in enumerate(pre):
        y = yns[i]
        for h, mh in enumerate(head_masks):
            y = y + mh * (_dot(r_bs[i][h], us[i]) + _dot(r_ks[i][h], p["v"]))
        decay = jnp.transpose(jnp.broadcast_to(jnp.exp(p["tot"]), (RW_LANES, RW_LANES)))
        m_new = decay * ms[i] + jnp.where(
            same_head, _dot_tn(p["b_out"], us[i]) + _dot_tn(p["k_out"], p["v"]), 0.0)
        out.append((y, m_new))
    return out


def _rwkv_chunk_kernel(r_ref, lw_ref, k_ref, v_ref, kk_ref, bb_ref, m0_ref,
                       y_ref, mfin_ref, m_scr, *, reverse):
    c_idx = pl.program_id(2)
    n_chunks = pl.num_programs(2)

    @pl.when(c_idx == 0)
    def _():
        m_scr[...] = m0_ref[0]

    lanes = [slice(p * RW_LANES, (p + 1) * RW_LANES) for p in range(RW_PAIRS_PER_STEP)]
    seqs = [tuple(ref[0, :, sl] for ref in (r_ref, lw_ref, k_ref, v_ref, kk_ref, bb_ref)) for sl in lanes]
    results = _rwkv_chunk_pairs(seqs, [m_scr[p] for p in range(RW_PAIRS_PER_STEP)], reverse)
    for p, (sl, (y, m_new)) in enumerate(zip(lanes, results)):
        y_ref[0, :, sl] = y
        m_scr[p] = m_new

    @pl.when(c_idx == n_chunks - 1)
    def _():
        mfin_ref[0] = m_scr[...]


def rwkv_chunked(r, lw, k, v, kk, bb, s0, reverse, interpret=False):
    b_, l, d = r.shape
    n_pairs = d // RW_LANES
    n_chunks = l // RW_CHUNK
    hn = RW_HEAD
    st = jnp.swapaxes(s0, -1, -2).reshape(b_, n_pairs, 2, hn, hn)
    zero = jnp.zeros_like(st[:, :, 0])
    m0 = jnp.concatenate([jnp.concatenate([st[:, :, 0], zero], -1),
                          jnp.concatenate([zero, st[:, :, 1]], -1)], -2)
    if reverse:
        seq_map = lambda b, p, c: (b, n_chunks - 1 - c, p)
    else:
        seq_map = lambda b, p, c: (b, c, p)
    pps = RW_PAIRS_PER_STEP
    seq_spec = pl.BlockSpec((1, RW_CHUNK, pps * RW_LANES), seq_map)
    st_spec = pl.BlockSpec((1, pps, RW_LANES, RW_LANES), lambda b, p, c: (b, p, 0, 0))
    y, mfin = pl.pallas_call(
        functools.partial(_rwkv_chunk_kernel, reverse=reverse),
        grid=(b_, n_pairs // pps, n_chunks),
        in_specs=[seq_spec] * 6 + [st_spec],
        out_specs=[seq_spec, st_spec],
        out_shape=[jax.ShapeDtypeStruct((b_, l, d), jnp.float32),
                   jax.ShapeDtypeStruct((b_, n_pairs, RW_LANES, RW_LANES), jnp.float32)],
        scratch_shapes=[pltpu.VMEM((pps, RW_LANES, RW_LANES), jnp.float32)],
        compiler_params=pltpu.CompilerParams(
            dimension_semantics=("parallel", "parallel", "arbitrary")),
        name="rwkv_scan_bwd" if reverse else "rwkv_scan_fwd",
        interpret=interpret,
    )(r, lw, k, v, kk, bb, m0)
    s_fin = jnp.stack([mfin[:, :, :hn, :hn], mfin[:, :, hn:, hn:]], 2)
    return y, jnp.swapaxes(s_fin.reshape(b_, 2 * n_pairs, hn, hn), -1, -2)


PEER_TOK = 16
PEER_SEL = PEER_HEADS * PEER_TOPK
PEER_ROWS = PEER_TOK * PEER_SEL
SUBLANES = 8
LANES = 128
PEER_GROUPS = PEER_ROWS // SUBLANES
PEER_DTILES = D_MODEL // LANES
PEER_RB = 128


def _peer_issue_group(idx_ref, uv_hbm, buf, sem, slot, grp):
    for sub in range(SUBLANES):
        e = idx_ref[0, 0, grp * SUBLANES + sub]
        pltpu.make_async_copy(uv_hbm.at[e], buf.at[slot, grp, :, sub, :],
                              sem.at[slot]).start(priority=sub % 2)


def _peer_wait_slot(buf, sem, slot):
    pltpu.make_async_copy(buf.at[1 - slot], buf.at[slot], sem.at[slot]).wait()


def _peer_expert_kernel(idx_first_ref, idx_next_ref, h_ref, g_ref, x_ref, og_ref, uv_hbm, o_ref, buf, sem):
    i = pl.program_id(0)
    n = pl.num_programs(0)

    @pl.when(i == 0)
    def _():
        def body(grp, carry):
            _peer_issue_group(idx_first_ref, uv_hbm, buf, sem, 0, grp)
            return carry
        lax.fori_loop(0, PEER_GROUPS, body, 0)

    bf16 = jnp.bfloat16
    n_rb = PEER_ROWS // PEER_RB
    grp_rb = PEER_RB // SUBLANES
    pieces = 2 * PEER_DTILES * n_rb
    per_piece = PEER_GROUPS // pieces

    def step(slot):
        _peer_wait_slot(buf, sem, slot)

        def issue_piece(piece):
            for grp in range(piece * per_piece, (piece + 1) * per_piece):
                _peer_issue_group(idx_next_ref, uv_hbm, buf, sem, 1 - slot, grp)

        def half(rb, c, low):
            x = buf[slot, rb * grp_rb:(rb + 1) * grp_rb, c, :, :].reshape(PEER_RB, LANES)
            x = (x << 16) if low else (x & jnp.uint32(0xFFFF0000))
            return lax.bitcast_convert_type(x, jnp.float32).astype(bf16)

        h = h_ref[...].astype(bf16)
        piece = 0
        blocks = []
        for rb in range(n_rb):
            sc = jnp.zeros((PEER_TOK, PEER_RB), jnp.float32)
            for c in range(PEER_DTILES):
                sc = sc + _dot_nt(h[:, c * LANES:(c + 1) * LANES], half(rb, c, False))
                issue_piece(piece)
                piece += 1
            blocks.append(sc)
        scores = jnp.concatenate(blocks, axis=1)
        tok = lax.broadcasted_iota(jnp.int32, (PEER_TOK, PEER_ROWS), 0)
        owner = lax.broadcasted_iota(jnp.int32, (PEER_TOK, PEER_ROWS), 1) // PEER_SEL
        own = tok == owner
        s_row = jnp.sum(jnp.where(own, scores, 0.0), axis=0, keepdims=True)
        act = 0.5 * s_row * (1.0 + lax.erf(s_row * (2.0 ** -0.5))) * g_ref[0]
        a_mat = jnp.where(own, jnp.broadcast_to(act, (PEER_TOK, PEER_ROWS)), 0.0).astype(bf16)
        for c in range(PEER_DTILES):
            acc = jnp.zeros((PEER_TOK, LANES), jnp.float32)
            for rb in range(n_rb):
                acc = acc + _dot(a_mat[:, rb * PEER_RB:(rb + 1) * PEER_RB], half(rb, c, True))
                issue_piece(piece)
                piece += 1
            sl = slice(c * LANES, (c + 1) * LANES)
            o_ref[:, sl] = x_ref[:, sl] + og_ref[0, :, sl] * acc

        @pl.when(i == n - 1)
        def _():
            _peer_wait_slot(buf, sem, 1 - slot)

    step(i % 2)


def pack_expert_table(u, v):
    bits = lambda t: lax.bitcast_convert_type(t.astype(jnp.bfloat16), jnp.uint16).astype(jnp.uint32)
    return ((bits(u) << 16) | bits(v)).reshape(u.shape[0], PEER_DTILES, LANES)


def peer_experts(h, e, g, x, out_gate, uv, rows_per_batch, interpret=False):
    n, d = h.shape
    steps = n // PEER_TOK
    idx = e.reshape(steps, 1, PEER_ROWS)
    gate = g.reshape(steps, 1, PEER_ROWS)
    return pl.pallas_call(
        _peer_expert_kernel,
        grid=(steps,),
        in_specs=[
            pl.BlockSpec((1, 1, PEER_ROWS), lambda i: (0, 0, 0), memory_space=pltpu.SMEM),
            pl.BlockSpec((1, 1, PEER_ROWS), lambda i: (jnp.minimum(i + 1, steps - 1), 0, 0),
                         memory_space=pltpu.SMEM),
            pl.BlockSpec((PEER_TOK, d), lambda i: (i, 0)),
            pl.BlockSpec((1, 1, PEER_ROWS), lambda i: (i, 0, 0)),
            pl.BlockSpec((PEER_TOK, d), lambda i: (i, 0)),
            _per_batch_spec(out_gate, rows_per_batch, PEER_TOK),
            pl.BlockSpec(memory_space=pl.ANY),
        ],
        out_specs=pl.BlockSpec((PEER_TOK, d), lambda i: (i, 0)),
        out_shape=jax.ShapeDtypeStruct((n, d), jnp.float32),
        scratch_shapes=[pltpu.VMEM((2, PEER_GROUPS, PEER_DTILES, SUBLANES, LANES), jnp.uint32),
                        pltpu.SemaphoreType.DMA((2,))],
        compiler_params=pltpu.CompilerParams(
            dimension_semantics=("arbitrary",),
            vmem_limit_bytes=40 * 1024 * 1024),
        name="peer_experts",
        interpret=interpret,
    )(idx, idx, h, gate, x, out_gate, uv)


ROUTE_TOK = 128


def _top_rounds(s, rounds):
    n_rows, t = s.shape
    row = lax.broadcasted_iota(jnp.int32, (n_rows, t), 0)
    out_row = lax.broadcasted_iota(jnp.int32, (rounds, t), 0)
    vals = jnp.zeros((rounds, t), jnp.float32)
    idxs = jnp.zeros((rounds, t), jnp.int32)
    for r in range(rounds):
        m = jnp.max(s, axis=0, keepdims=True)
        i = jnp.min(jnp.where(s == m, row, n_rows), axis=0, keepdims=True)
        vals = jnp.where(out_row == r, m, vals)
        idxs = jnp.where(out_row == r, i, idxs)
        s = jnp.where(row == i, -jnp.inf, s)
    return vals, idxs


def _top_pair_sums(x, y):
    k, t = x.shape
    assert k == 16
    sub = lax.broadcasted_iota(jnp.int32, (SUBLANES, t), 0)
    blocks = [x[0:1, :] + y]
    for a in range(1, 8):
        blocks.append(jnp.where(sub < k // (a + 1), x[a:a + 1, :] + y[0:SUBLANES, :], -jnp.inf))
    blocks.append(x[8:16, :] + y[0:1, :])
    vals, r = _top_rounds(jnp.concatenate(blocks, axis=0), k)
    blk = r // SUBLANES
    a_idx = jnp.where(r < k, 0, jnp.where(blk < 9, blk - 1, r - 64))
    b_idx = jnp.where(r < k, r, jnp.where(blk < 9, r % SUBLANES, 0))
    return vals, a_idx, b_idx


def _select_rows(table, sel):
    k, t = table.shape
    out = jnp.zeros(sel.shape, table.dtype)
    for j in range(k):
        out = jnp.where(sel == j, table[j:j + 1, :], out)
    return out


def _peer_route_kernel(x_ref, nw_ref, shift_ref, scale_ref, wq_ref, keys_ref, h_ref, e_ref, g_ref):
    x = x_ref[...]
    h = x * lax.rsqrt(jnp.mean(x * x, -1, keepdims=True) + NORM_EPS) * nw_ref[...]
    h = h * (1 + scale_ref[0]) + shift_ref[0]
    h_ref[...] = h
    q = _dot(h, wq_ref[...])
    e_rows, g_rows = [], []
    for hd in range(PEER_HEADS):
        sv, si = [], []
        for p in range(2):
            c0 = (hd * 2 + p) * PEER_HALF
            s_t = _dot_nt(keys_ref[hd, p], q[:, c0:c0 + PEER_HALF])
            v_, i_ = _top_rounds(s_t, PEER_TOPK)
            sv.append(v_)
            si.append(i_)
        fv, fa, fb = _top_pair_sums(sv[0], sv[1])
        i1 = _select_rows(si[0], fa)
        i2 = _select_rows(si[1], fb)
        e_rows.append(i1 * PEER_NKEYS + i2)
        ex = jnp.exp(fv - jnp.max(fv, axis=0, keepdims=True))
        g_rows.append(ex / jnp.sum(ex, axis=0, keepdims=True))
    e_ref[...] = jnp.concatenate(e_rows, axis=0).T
    g_ref[...] = jnp.concatenate(g_rows, axis=0).T


def _per_batch_spec(vec, rows_per_batch, tile):
    if vec.shape[0] == 1:
        return pl.BlockSpec((1, 1, vec.shape[2]), lambda i: (0, 0, 0))
    return pl.BlockSpec((1, 1, vec.shape[2]), lambda i: (i // (rows_per_batch // tile), 0, 0))


def peer_route(x, norm_w, shift, scale, wq, keys, rows_per_batch, interpret=False):
    n, d = x.shape
    return pl.pallas_call(
        _peer_route_kernel,
        grid=(n // ROUTE_TOK,),
        in_specs=[pl.BlockSpec((ROUTE_TOK, d), lambda i: (i, 0)),
                  pl.BlockSpec((1, d), lambda i: (0, 0)),
                  _per_batch_spec(shift, rows_per_batch, ROUTE_TOK),
                  _per_batch_spec(scale, rows_per_batch, ROUTE_TOK),
                  pl.BlockSpec(wq.shape, lambda i: (0, 0)),
                  pl.BlockSpec(keys.shape, lambda i: (0, 0, 0, 0))],
        out_specs=[pl.BlockSpec((ROUTE_TOK, d), lambda i: (i, 0)),
                   pl.BlockSpec((ROUTE_TOK, PEER_SEL), lambda i: (i, 0)),
                   pl.BlockSpec((ROUTE_TOK, PEER_SEL), lambda i: (i, 0))],
        out_shape=[jax.ShapeDtypeStruct((n, d), jnp.float32),
                   jax.ShapeDtypeStruct((n, PEER_SEL), jnp.int32),
                   jax.ShapeDtypeStruct((n, PEER_SEL), jnp.float32)],
        compiler_params=pltpu.CompilerParams(
            dimension_semantics=("parallel",), vmem_limit_bytes=40 * 1024 * 1024),
        name="peer_route",
        interpret=interpret,
    )(x, norm_w, shift, scale, wq, keys)


DENSE_TOK = 512
DENSE_COLS = 1024


def _dense_kernel(*refs, mixed, biased):
    x_ref = refs[0]
    x = x_ref[...]
    pos = 1
    if mixed:
        x = x + refs[1][...] * refs[2][...]
        pos = 3
    w_ref = refs[pos]
    o_ref = refs[-1]
    acc = _dot(x.astype(jnp.bfloat16), w_ref[...])
    if biased:
        acc = acc + refs[pos + 1][...]
    o_ref[...] = acc


def dense(x, w, b=None, mix=None, interpret=False):
    n, k = x.shape
    m = w.shape[1]
    cols = min(DENSE_COLS, m)
    tok = min(DENSE_TOK, n)
    row_spec = pl.BlockSpec((tok, k), lambda j, i: (i, 0))
    args, specs = [x], [row_spec]
    if mix is not None:
        args += [mix[0], mix[1].reshape(1, k)]
        specs += [row_spec, pl.BlockSpec((1, k), lambda j, i: (0, 0))]
    args.append(w.astype(jnp.bfloat16))
    specs.append(pl.BlockSpec((k, cols), lambda j, i: (0, j)))
    if b is not None:
        args.append(b.reshape(1, m))
        specs.append(pl.BlockSpec((1, cols), lambda j, i: (0, j)))
    return pl.pallas_call(
        functools.partial(_dense_kernel, mixed=mix is not None, biased=b is not None),
        grid=(m // cols, n // tok),
        in_specs=specs,
        out_specs=pl.BlockSpec((tok, cols), lambda j, i: (i, j)),
        out_shape=jax.ShapeDtypeStruct((n, m), jnp.float32),
        compiler_params=pltpu.CompilerParams(dimension_semantics=("parallel", "parallel")),
        name="dense",
        interpret=interpret,
    )(*args)


def dense3(x, w, b=None, mix=None):
    b_, l, k = x.shape
    mix2 = None if mix is None else (mix[0].reshape(b_ * l, k), mix[1])
    return dense(x.reshape(b_ * l, k), w, b, mix2).reshape(b_, l, w.shape[1])


def rmsnorm(x, g):
    xf = x.astype(jnp.float32)
    y = xf * lax.rsqrt(jnp.mean(xf * xf, -1, keepdims=True) + NORM_EPS)
    return y.astype(x.dtype) * g


def modulate(x, g, shift, scale):
    return rmsnorm(x, g) * (1 + scale) + shift


def shift_grid(h):
    b_, l, d = h.shape
    rows = l // GRID_W
    g = h.reshape(b_, rows, GRID_W, d)
    q = d // 4
    left = jnp.pad(g[:, :, :-1, :q], ((0, 0), (0, 0), (1, 0), (0, 0)))
    right = jnp.pad(g[:, :, 1:, q:2 * q], ((0, 0), (0, 0), (0, 1), (0, 0)))
    up = jnp.pad(g[:, :-1, :, 2 * q:3 * q], ((0, 0), (1, 0), (0, 0), (0, 0)))
    down = jnp.pad(g[:, 1:, :, 3 * q:], ((0, 0), (0, 1), (0, 0), (0, 0)))
    return jnp.concatenate([left, right, up, down], -1).reshape(b_, l, d)


def shift_seq(h):
    d = h.shape[-1] // 2
    prev = jnp.pad(h[:, :-1, :d], ((0, 0), (1, 0), (0, 0)))
    nxt = jnp.pad(h[:, 1:, d:], ((0, 0), (0, 1), (0, 0)))
    return jnp.concatenate([prev, nxt], -1)


def rwkv_mixer(hc, hl, need_ctx, p):
    (mu, w_rkv, w0, w1, w2, a0, a1, a2, g1, g2, k_k, k_a, r_k, ln_w, ln_b, w_o) = p
    f32 = jnp.float32

    def run(h, shifted, s_init):
        b_, l, d = h.shape
        heads = lambda t: t.reshape(b_, l, RW_HEADS, RW_HEAD).astype(f32)
        xx = shifted - h
        r, k, v = (dense3(h, w_rkv[j_], mix=(xx, mu[j_])) for j_ in range(3))
        xw, xa, xg = h + xx * mu[3], h + xx * mu[4], h + xx * mu[5]
        kk = heads(k * k_k)
        kk = (kk * lax.rsqrt(jnp.sum(kk * kk, -1, keepdims=True) + 1e-12)).reshape(b_, l, d)
        y = jnp.zeros_like(v)
        k_sum = jnp.zeros_like(v)
        finals = []
        for dr in range(2):
            wlog = -jax.nn.softplus(-(w0[dr] + jnp.tanh(xw @ w1[dr]) @ w2[dr]).astype(f32)) - 0.5
            a = jax.nn.sigmoid((a0[dr] + (xa @ a1[dr]) @ a2[dr]).astype(f32))
            kd = (k * (1 + (a - 1) * k_a)).astype(f32)
            ys, sf = rwkv_chunked(r, -jnp.exp(wlog), kd, v, kk, kk * a, s_init[dr], dr == 1)
            y = y + ys
            k_sum = k_sum + kd
            finals.append(sf)
        return (heads(y), heads(r), heads(v), heads(k_sum), xg), finals

    def post(h, y, r, v, k_sum, xg):
        b_, l, d = h.shape
        mean = jnp.mean(y, -1, keepdims=True)
        var = jnp.mean(jnp.square(y - mean), -1, keepdims=True)
        yn = ((y - mean) * lax.rsqrt(var + RW_GN_EPS)).reshape(b_, l, d) * ln_w + ln_b
        bonus = jnp.sum(r * k_sum * r_k.reshape(RW_HEADS, RW_HEAD).astype(f32), -1, keepdims=True) * v
        g = jax.nn.sigmoid(xg @ g1) @ g2
        return dense3((yn + bonus.reshape(b_, l, d)).astype(h.dtype) * g, w_o)

    s_zero = jnp.zeros((hc.shape[0], RW_HEADS, RW_HEAD, RW_HEAD), f32)
    out_c, s_ctx = run(hc, shift_seq(hc), (s_zero, s_zero))
    out_l, _ = run(hl, shift_grid(hl), s_ctx)
    yc = post(hc, *out_c) if need_ctx else None
    return yc, post(hl, *out_l)


def _gla_chunk_kernel(qf_ref, kf_ref, vf_ref, ldf_ref, qb_ref, kb_ref, vb_ref, ldb_ref, s0f_ref, s0b_ref,
                      of_ref, ob_ref, sff_ref, sfb_ref, sf_scr, sb_scr):
    f32 = jnp.float32
    C = GLA_CHUNK
    c_idx = pl.program_id(2)
    n_chunks = pl.num_programs(2)

    @pl.when(c_idx == 0)
    def _():
        sf_scr[...] = s0f_ref[0, 0]
        sb_scr[...] = s0b_ref[0, 0]

    row = lax.broadcasted_iota(jnp.int32, (C, C), 0)
    col = lax.broadcasted_iota(jnp.int32, (C, C), 1)
    incls = [row >= col, row <= col]
    qs = [qf_ref[0], qb_ref[0]]
    ks = [kf_ref[0], kb_ref[0]]
    vs = [vf_ref[0], vb_ref[0]]
    lds = [ldf_ref[0], ldb_ref[0]]
    states = [sf_scr[...], sb_scr[...]]
    cums = [_dot(jnp.where(m, 1.0, 0.0).astype(f32), ld, _HI) for m, ld in zip(incls, lds)]
    mids = [cum[C // 2:C // 2 + 1, :] for cum in cums]
    lasts = [cums[0][C - 1:C, :], cums[1][0:1, :]]
    scores = [jnp.where(m, _dot_nt(q * jnp.exp(cum - mid), k * jnp.exp(mid - cum)), 0.0)
              for m, q, k, cum, mid in zip(incls, qs, ks, cums, mids)]
    outs = [_dot(sc, v) + _dot(q * jnp.exp(cum), s)
            for sc, v, q, cum, s in zip(scores, vs, qs, cums, states)]
    decays = [jnp.transpose(jnp.broadcast_to(jnp.exp(last), (GLA_DK, GLA_DK))) for last in lasts]
    new_states = [jnp.concatenate([dec] * (GLA_DV // GLA_DK), axis=1) * s + _dot_tn(k * jnp.exp(last - cum), v)
                  for dec, s, k, last, cum, v in zip(decays, states, ks, lasts, cums, vs)]
    of_ref[0] = outs[0]
    ob_ref[0] = outs[1]
    sf_scr[...] = new_states[0]
    sb_scr[...] = new_states[1]

    @pl.when(c_idx == n_chunks - 1)
    def _():
        sff_ref[0, 0] = new_states[0]
        sfb_ref[0, 0] = new_states[1]


def gla_scan(q, k, v, ld_f, ld_b, s0_f, s0_b, interpret=False):
    b_, l, _ = q.shape
    n = l // GLA_CHUNK
    fwd = lambda b, h, c: (b, c, h)
    bwd = lambda b, h, c: (b, n - 1 - c, h)
    kspec = lambda m: pl.BlockSpec((1, GLA_CHUNK, GLA_DK), m)
    vspec = lambda m: pl.BlockSpec((1, GLA_CHUNK, GLA_DV), m)
    sspec = pl.BlockSpec((1, 1, GLA_DK, GLA_DV), lambda b, h, c: (b, h, 0, 0))
    o_f, o_b, sf, sb = pl.pallas_call(
        _gla_chunk_kernel,
        grid=(b_, GLA_HEADS, n),
        in_specs=[kspec(fwd), kspec(fwd), vspec(fwd), kspec(fwd),
                  kspec(bwd), kspec(bwd), vspec(bwd), kspec(bwd), sspec, sspec],
        out_specs=[vspec(fwd), vspec(bwd), sspec, sspec],
        out_shape=[jax.ShapeDtypeStruct(v.shape, jnp.float32), jax.ShapeDtypeStruct(v.shape, jnp.float32),
                   jax.ShapeDtypeStruct(s0_f.shape, jnp.float32), jax.ShapeDtypeStruct(s0_b.shape, jnp.float32)],
        scratch_shapes=[pltpu.VMEM((GLA_DK, GLA_DV), jnp.float32), pltpu.VMEM((GLA_DK, GLA_DV), jnp.float32)],
        compiler_params=pltpu.CompilerParams(
            dimension_semantics=("parallel", "parallel", "arbitrary")),
        name="gla_scan",
        interpret=interpret,
    )(q, k, v, ld_f, q, k, v, ld_b, s0_f, s0_b)
    return o_f + o_b, sf, sb


def gla_mixer(hc, hl, need_ctx, p):
    w_in, wg1, wg2, bg, norm_w, w_o = p
    f32 = jnp.float32
    hk = GLA_HEADS * GLA_DK

    def run(h, s_init):
        b_, l, d = h.shape
        q, k, v, og = jnp.split(dense3(h, w_in), [hk, 2 * hk, 2 * hk + d], -1)
        q = q.astype(f32) * GLA_DK ** -0.5
        logd = [jax.nn.log_sigmoid(((h @ wg1[dr]) @ wg2[dr] + bg[dr]).astype(f32)) / GLA_GATE_NORM
                for dr in range(2)]
        o, s_f, s_b = gla_scan(q, k.astype(f32), v.astype(f32), logd[0], logd[1], s_init[0], s_init[1])
        return o.reshape(b_, l, GLA_HEADS, GLA_DV), og, (s_f, s_b)

    def post(h, o, og):
        b_, l, d = h.shape
        on = o * lax.rsqrt(jnp.mean(o * o, -1, keepdims=True) + NORM_EPS) * norm_w
        return dense3(on.reshape(b_, l, d).astype(h.dtype) * jax.nn.silu(og), w_o)

    s_zero = jnp.zeros((hc.shape[0], GLA_HEADS, GLA_DK, GLA_DV), f32)
    oc, ogc, s_ctx = run(hc, (s_zero, s_zero))
    ol, ogl, _ = run(hl, s_ctx)
    yc = post(hc, oc, ogc) if need_ctx else None
    return yc, post(hl, ol, ogl)


def conv3(z, w, b):
    zp = jnp.pad(z, ((0, 0), (1, 1), (0, 0)))
    return zp[:, :-2] * w[0] + zp[:, 1:-1] * w[1] + zp[:, 2:] * w[2] + b


def hyena_filters(l, f1, fb1, f2, fb2, f3, fb3, f4, freq):
    f32 = jnp.float32
    t = jnp.linspace(0.0, 1.0, l, dtype=f32)[:, None]
    w = 2 * math.pi * jnp.arange(l, dtype=f32)[:, None] / l
    f = jnp.linspace(1e-4, HY_BANDS - 1, HY_BANDS, dtype=f32)[None]
    z = jnp.concatenate([t, jnp.cos(f * w), -jnp.sin(f * w)], -1)
    fr = freq.astype(f32)
    hdn = jnp.sin(fr * (z @ f1.astype(f32) + fb1.astype(f32)))
    hdn = jnp.sin(fr * (hdn @ f2.astype(f32) + fb2.astype(f32)))
    hdn = jnp.sin(fr * (hdn @ f3.astype(f32) + fb3.astype(f32)))
    filt = (hdn @ f4.astype(f32)).reshape(l, HY_ORDER, 2, D_MODEL)
    min_decay = math.log(HY_TARGET) / HY_SLOW_DECAY
    max_decay = math.log(HY_TARGET) / HY_FAST_DECAY
    deltas = jnp.linspace(min_decay, max_decay, D_MODEL, dtype=f32)
    filt = filt * jnp.exp(-t * jnp.abs(deltas))[:, None, None, :]
    fwd, bwd = filt[:, :, 0], filt[:, :, 1]
    kc = jnp.concatenate([fwd, jnp.zeros_like(fwd[:1]), jnp.flip(bwd[1:], 0)], 0)
    return kc / jnp.sum(jnp.abs(kc), 0, keepdims=True)


def long_conv(u, kc_o, bias_o):
    b_, l, _ = u.shape
    half = b_ // 2
    z = lax.complex(u[:half], u[half:])
    kf = jnp.fft.fft(kc_o.astype(jnp.complex64), n=2 * l, axis=0)
    out = jnp.fft.ifft(jnp.fft.fft(z, n=2 * l, axis=1) * kf[None], n=2 * l, axis=1)[:, :l]
    return jnp.concatenate([jnp.real(out), jnp.imag(out)], axis=0) + u * bias_o


def hyena_mixer(h, p):
    (w_in, b_in, sc_w, sc_b, f1, fb1, f2, fb2, f3, fb3, f4, freq, bias, w_o, b_o) = p
    l = h.shape[1]
    z = conv3(dense3(h, w_in, b_in), sc_w, sc_b).astype(jnp.float32)
    v, x1, x2 = jnp.split(z, 3, -1)
    kc = hyena_filters(l, f1, fb1, f2, fb2, f3, fb3, f4, freq)
    bias = bias.astype(jnp.float32)
    y = x1 * long_conv(v, kc[:, 0], bias[0])
    y = x2 * long_conv(y, kc[:, 1], bias[1])
    return dense3(y.astype(h.dtype), w_o, b_o)


def peer_layer(x, norm_w, shift, scale, out_gate, p):
    wq, keys, uv = p
    b_, l, d = x.shape
    xf = x.reshape(b_ * l, d)
    h, e, g = peer_route(xf, norm_w.reshape(1, d), shift, scale, wq, keys, l)
    return peer_experts(h, e, g, xf, out_gate, uv, l).reshape(b_, l, d)


def kernel(x, c, ctx, c_ctx, ada_w, ada_b, norm_mix, norm_ffn, norm_out,
           rw_mu, rw_wrkv, rw_w0, rw_w1, rw_w2, rw_a0, rw_a1, rw_a2, rw_g1, rw_g2,
           rw_kk, rw_ka, rw_rk, rw_lnw, rw_lnb, rw_wo,
           gla_win, gla_wg1, gla_wg2, gla_bg, gla_normw, gla_wo,
           hy_win, hy_bin, hy_scw, hy_scb, hy_f1, hy_fb1, hy_f2, hy_fb2, hy_f3, hy_fb3,
           hy_f4, hy_freq, hy_bias, hy_wo, hy_bo,
           pe_wq, pe_keys, pe_u, pe_v):
    rw = lambda j: (rw_mu[j], rw_wrkv[j], rw_w0[j], rw_w1[j], rw_w2[j], rw_a0[j], rw_a1[j], rw_a2[j],
                    rw_g1[j], rw_g2[j], rw_kk[j], rw_ka[j], rw_rk[j], rw_lnw[j], rw_lnb[j], rw_wo[j])
    gla = lambda j: (gla_win[j], gla_wg1[j], gla_wg2[j], gla_bg[j], gla_normw[j], gla_wo[j])
    hy = lambda j: (hy_win[j], hy_bin[j], hy_scw[j], hy_scb[j], hy_f1[j], hy_fb1[j], hy_f2[j], hy_fb2[j],
                    hy_f3[j], hy_fb3[j], hy_f4[j], hy_freq[j], hy_bias[j], hy_wo[j], hy_bo[j])
    pe_uv = [pack_expert_table(pe_u[i], pe_v[i]) for i in range(DEPTH)]
    pe = lambda i: (pe_wq[i], pe_keys[i], pe_uv[i])

    xl, xc = x, ctx
    cond = jax.nn.silu(c)
    cond_ctx = jax.nn.silu(c_ctx)
    for i in range(DEPTH):
        kind, j = i % N_MIXERS, i // N_MIXERS
        last = i == DEPTH - 1
        mod = (cond @ ada_w[i] + ada_b[i]).reshape(-1, 1, 6, D_MODEL)
        mod_c = (cond_ctx @ ada_w[i] + ada_b[i]).reshape(1, 1, 6, D_MODEL)
        hl = modulate(xl, norm_mix[i], mod[:, :, 0], mod[:, :, 1])
        if kind == 2:
            yl = hyena_mixer(hl, hy(j))
            yc = None if last else hyena_mixer(modulate(xc, norm_mix[i], mod_c[:, :, 0], mod_c[:, :, 1]), hy(j))
        else:
            hc = modulate(xc, norm_mix[i], mod_c[:, :, 0], mod_c[:, :, 1])
            if kind == 0:
                yc, yl = rwkv_mixer(hc, hl, not last, rw(j))
            else:
                yc, yl = gla_mixer(hc, hl, not last, gla(j))
        xl = xl + mod[:, :, 2] * yl
        xl = peer_layer(xl, norm_ffn[i], mod[:, :, 3], mod[:, :, 4], mod[:, :, 5], pe(i))
        if not last:
            xc = xc + mod_c[:, :, 2] * yc
            xc = peer_layer(xc, norm_ffn[i], mod_c[:, :, 3], mod_c[:, :, 4], mod_c[:, :, 5], pe(i))
    return _final_norm(xl, norm_out)
```

```python
import math
import functools
import jax, jax.numpy as jnp
from jax import lax
from jax.experimental import pallas as pl
from jax.experimental.pallas import tpu as pltpu

D_MODEL = 1024
BATCH = 8
SEQ = 8192
DEPTH = 4

GRID_W = 64
CTX_LEN = 256
N_MIXERS = 3
NORM_EPS = 1e-6

RW_HEAD = 64
RW_HEADS = D_MODEL // RW_HEAD
RW_GN_EPS = 64e-5

GLA_HEADS = 4
GLA_DK = D_MODEL // 2 // GLA_HEADS
GLA_DV = D_MODEL // GLA_HEADS
GLA_GATE_NORM = 16.0
GLA_CHUNK = 64

HY_ORDER = 2
HY_EMB = 33
HY_BANDS = (HY_EMB - 1) // 2
HY_FAST_DECAY = 0.3
HY_SLOW_DECAY = 1.5
HY_TARGET = 1e-2

PEER_HEADS = 8
PEER_NKEYS = 128
PEER_QDIM = 256
PEER_HALF = PEER_QDIM // 2
PEER_TOPK = 16
PEER_BLOCK = 128


def _final_norm_kernel(x_ref, g_ref, o_ref):
    x = x_ref[...]
    y = x * lax.rsqrt(jnp.mean(x * x, -1, keepdims=True) + NORM_EPS)
    o_ref[...] = y * g_ref[...]


def _final_norm(x, g):
    b_, l, d = x.shape
    rows = b_ * l
    tile = 1024
    out = pl.pallas_call(
        _final_norm_kernel,
        grid=(rows // tile,),
        in_specs=[pl.BlockSpec((tile, d), lambda i: (i, 0)),
                  pl.BlockSpec((1, d), lambda i: (0, 0))],
        out_specs=pl.BlockSpec((tile, d), lambda i: (i, 0)),
        out_shape=jax.ShapeDtypeStruct((rows, d), x.dtype),
        name="final_norm",
    )(x.reshape(rows, d), g.reshape(1, d))
    return out.reshape(b_, l, d)


RW_CHUNK = 64
RW_LANES = 128
RW_PAIRS_PER_STEP = 8
_HI = lax.Precision.HIGHEST


def _dot(x, y, precision=None):
    return jnp.dot(x, y, precision=precision, preferred_element_type=jnp.float32)


def _dot_nt(x, y, precision=None):
    return lax.dot_general(x, y, (((1,), (1,)), ((), ())), precision=precision,
                           preferred_element_type=jnp.float32)


def _dot_tn(x, y, precision=None):
    return lax.dot_general(x, y, (((0,), (0,)), ((), ())), precision=precision,
                           preferred_element_type=jnp.float32)


def _unit_triangular_inverses(mats):
    c = mats[0].shape[0]
    row = lax.broadcasted_iota(jnp.int32, (c, 2 * c), 0)
    col = lax.broadcasted_iota(jnp.int32, (c, 2 * c), 1)
    left = col < c
    xs = [jnp.where(left, jnp.concatenate([a, a], axis=1), jnp.where(col - c == row, 1.0, 0.0)) for a in mats]
    steps = c.bit_length() - 1
    for _ in range(steps):
        xs = [_dot(x[:, :c], x, _HI) + jnp.where(left, 0.0, x) for x in xs]
    return xs


def _rwkv_chunk_pairs(seqs, ms, reverse):
    f32 = jnp.float32
    C = RW_CHUNK
    row = lax.broadcasted_iota(jnp.int32, (C, C), 0)
    col = lax.broadcasted_iota(jnp.int32, (C, C), 1)
    if reverse:
        incl, strict = row <= col, row < col
    else:
        incl, strict = row >= col, row > col
    tri = jnp.where(incl, 1.0, 0.0).astype(f32)
    lane = lax.broadcasted_iota(jnp.int32, (1, RW_LANES), 1)
    head_masks = [jnp.where(lane < RW_HEAD, 1.0, 0.0).astype(f32),
                  jnp.where(lane >= RW_HEAD, 1.0, 0.0).astype(f32)]
    ri = lax.broadcasted_iota(jnp.int32, (RW_LANES, RW_LANES), 0)
    ci = lax.broadcasted_iota(jnp.int32, (RW_LANES, RW_LANES), 1)
    same_head = (ri < RW_HEAD) == (ci < RW_HEAD)
    n = len(seqs)

    cums = [_dot(tri, s[1], _HI) for s in seqs]
    pre = []
    for (r, lw, k, v, kk, bb), cum in zip(seqs, cums):
        mid = cum[C // 2:C // 2 + 1, :]
        tot = cum[0:1, :] if reverse else cum[C - 1:C, :]
        e_inv = jnp.exp(mid - cum)
        e_out = jnp.exp(tot - cum)
        pre.append(dict(
            tot=tot, v=v,
            a_rel=-kk * jnp.exp(cum - lw - mid), b_rel=bb * e_inv, k_rel=k * e_inv,
            r_rel=r * jnp.exp(cum - mid), a_abs=-kk * jnp.exp(cum - lw), r_abs=r * jnp.exp(cum),
            k_out=k * e_out, b_out=bb * e_out))
    uns = [_dot(p["a_abs"], m) for p, m in zip(pre, ms)]
    yns = [_dot(p["r_abs"], m) for p, m in zip(pre, ms)]
    a_hs = [[p["a_rel"] * mh for mh in head_masks] for p in pre]
    a_abs_ = [[jnp.where(strict, _dot_nt(a_h, p["b_rel"]), 0.0) for a_h in a_hs[i]] for i, p in enumerate(pre)]
    a_aks = [[jnp.where(strict, _dot_nt(a_h, p["k_rel"]), 0.0) for a_h in a_hs[i]] for i, p in enumerate(pre)]
    rhss = [[uns[i] + _dot(a_ak, p["v"]) for a_ak in a_aks[i]] for i, p in enumerate(pre)]
    invs = _unit_triangular_inverses([a for pair in a_abs_ for a in pair])
    us = []
    for i in range(n):
        parts = [_dot(invs[2 * i + h][:, C:], rhss[i][h], _HI) for h in range(2)]
        us.append(head_masks[0] * parts[0] + head_masks[1] * parts[1])
    r_hs = [[p["r_rel"] * mh for mh in head_masks] for p in pre]
    r_bs = [[jnp.where(incl, _dot_nt(r_h, p["b_rel"]), 0.0) for r_h in r_hs[i]] for i, p in enumerate(pre)]
    r_ks = [[jnp.where(incl, _dot_nt(r_h, p["k_rel"]), 0.0) for r_h in r_hs[i]] for i, p in enumerate(pre)]
    out = []
    for i, p in enumerate(pre):
        y = yns[i]
        for h, mh in enumerate(head_masks):
            y = y + mh * (_dot(r_bs[i][h], us[i]) + _dot(r_ks[i][h], p["v"]))
        decay = jnp.transpose(jnp.broadcast_to(jnp.exp(p["tot"]), (RW_LANES, RW_LANES)))
        m_new = decay * ms[i] + jnp.where(
            same_head, _dot_tn(p["b_out"], us[i]) + _dot_tn(p["k_out"], p["v"]), 0.0)
        out.append((y, m_new))
    return out


def _rwkv_chunk_kernel(r_ref, lw_ref, k_ref, v_ref, kk_ref, bb_ref, m0_ref,
                       y_ref, mfin_ref, m_scr, *, reverse):
    c_idx = pl.program_id(2)
    n_chunks = pl.num_programs(2)

    @pl.when(c_idx == 0)
    def _():
        m_scr[...] = m0_ref[0]

    lanes = [slice(p * RW_LANES, (p + 1) * RW_LANES) for p in range(RW_PAIRS_PER_STEP)]
    seqs = [tuple(ref[0, :, sl] for ref in (r_ref, lw_ref, k_ref, v_ref, kk_ref, bb_ref)) for sl in lanes]
    results = _rwkv_chunk_pairs(seqs, [m_scr[p] for p in range(RW_PAIRS_PER_STEP)], reverse)
    for p, (sl, (y, m_new)) in enumerate(zip(lanes, results)):
        y_ref[0, :, sl] = y
        m_scr[p] = m_new

    @pl.when(c_idx == n_chunks - 1)
    def _():
        mfin_ref[0] = m_scr[...]


def rwkv_chunked(r, lw, k, v, kk, bb, s0, reverse, interpret=False):
    b_, l, d = r.shape
    n_pairs = d // RW_LANES
    n_chunks = l // RW_CHUNK
    hn = RW_HEAD
    st = jnp.swapaxes(s0, -1, -2).reshape(b_, n_pairs, 2, hn, hn)
    zero = jnp.zeros_like(st[:, :, 0])
    m0 = jnp.concatenate([jnp.concatenate([st[:, :, 0], zero], -1),
                          jnp.concatenate([zero, st[:, :, 1]], -1)], -2)
    if reverse:
        seq_map = lambda b, p, c: (b, n_chunks - 1 - c, p)
    else:
        seq_map = lambda b, p, c: (b, c, p)
    pps = RW_PAIRS_PER_STEP
    seq_spec = pl.BlockSpec((1, RW_CHUNK, pps * RW_LANES), seq_map)
    st_spec = pl.BlockSpec((1, pps, RW_LANES, RW_LANES), lambda b, p, c: (b, p, 0, 0))
    y, mfin = pl.pallas_call(
        functools.partial(_rwkv_chunk_kernel, reverse=reverse),
        grid=(b_, n_pairs // pps, n_chunks),
        in_specs=[seq_spec] * 6 + [st_spec],
        out_specs=[seq_spec, st_spec],
        out_shape=[jax.ShapeDtypeStruct((b_, l, d), jnp.float32),
                   jax.ShapeDtypeStruct((b_, n_pairs, RW_LANES, RW_LANES), jnp.float32)],
        scratch_shapes=[pltpu.VMEM((pps, RW_LANES, RW_LANES), jnp.float32)],
        compiler_params=pltpu.CompilerParams(
            dimension_semantics=("parallel", "parallel", "arbitrary")),
        name="rwkv_scan_bwd" if reverse else "rwkv_scan_fwd",
        interpret=interpret,
    )(r, lw, k, v, kk, bb, m0)
    s_fin = jnp.stack([mfin[:, :, :hn, :hn], mfin[:, :, hn:, hn:]], 2)
    return y, jnp.swapaxes(s_fin.reshape(b_, 2 * n_pairs, hn, hn), -1, -2)


PEER_TOK = 16
PEER_SEL = PEER_HEADS * PEER_TOPK
PEER_ROWS = PEER_TOK * PEER_SEL
SUBLANES = 8
LANES = 128
PEER_GROUPS = PEER_ROWS // SUBLANES
PEER_DTILES = D_MODEL // LANES
PEER_RB = 128


def _peer_issue_group(idx_ref, uv_hbm, buf, sem, slot, grp):
    for sub in range(SUBLANES):
        e = idx_ref[0, 0, grp * SUBLANES + sub]
        pltpu.make_async_copy(uv_hbm.at[e], buf.at[slot, grp, :, sub, :],
                              sem.at[slot]).start(priority=sub % 2)


def _peer_wait_slot(buf, sem, slot):
    pltpu.make_async_copy(buf.at[1 - slot], buf.at[slot], sem.at[slot]).wait()


def _peer_expert_kernel(idx_first_ref, idx_next_ref, h_ref, g_ref, x_ref, og_ref, uv_hbm, o_ref, buf, sem):
    i = pl.program_id(0)
    n = pl.num_programs(0)

    @pl.when(i == 0)
    def _():
        def body(grp, carry):
            _peer_issue_group(idx_first_ref, uv_hbm, buf, sem, 0, grp)
            return carry
        lax.fori_loop(0, PEER_GROUPS, body, 0)

    bf16 = jnp.bfloat16
    n_rb = PEER_ROWS // PEER_RB
    grp_rb = PEER_RB // SUBLANES
    pieces = 2 * PEER_DTILES * n_rb
    per_piece = PEER_GROUPS // pieces

    def step(slot):
        _peer_wait_slot(buf, sem, slot)

        def issue_piece(piece):
            for grp in range(piece * per_piece, (piece + 1) * per_piece):
                _peer_issue_group(idx_next_ref, uv_hbm, buf, sem, 1 - slot, grp)

        def half(rb, c, low):
            x = buf[slot, rb * grp_rb:(rb + 1) * grp_rb, c, :, :].reshape(PEER_RB, LANES)
            x = (x << 16) if low else (x & jnp.uint32(0xFFFF0000))
            return lax.bitcast_convert_type(x, jnp.float32).astype(bf16)

        h = h_ref[...].astype(bf16)
        piece = 0
        blocks = []
        for rb in range(n_rb):
            sc = jnp.zeros((PEER_TOK, PEER_RB), jnp.float32)
            for c in range(PEER_DTILES):
                sc = sc + _dot_nt(h[:, c * LANES:(c + 1) * LANES], half(rb, c, False))
                issue_piece(piece)
                piece += 1
            blocks.append(sc)
        scores = jnp.concatenate(blocks, axis=1)
        tok = lax.broadcasted_iota(jnp.int32, (PEER_TOK, PEER_ROWS), 0)
        owner = lax.broadcasted_iota(jnp.int32, (PEER_TOK, PEER_ROWS), 1) // PEER_SEL
        own = tok == owner
        s_row = jnp.sum(jnp.where(own, scores, 0.0), axis=0, keepdims=True)
        act = 0.5 * s_row * (1.0 + lax.erf(s_row * (2.0 ** -0.5))) * g_ref[0]
        a_mat = jnp.where(own, jnp.broadcast_to(act, (PEER_TOK, PEER_ROWS)), 0.0).astype(bf16)
        for c in range(PEER_DTILES):
            acc = jnp.zeros((PEER_TOK, LANES), jnp.float32)
            for rb in range(n_rb):
                acc = acc + _dot(a_mat[:, rb * PEER_RB:(rb + 1) * PEER_RB], half(rb, c, True))
                issue_piece(piece)
                piece += 1
            sl = slice(c * LANES, (c + 1) * LANES)
            o_ref[:, sl] = x_ref[:, sl] + og_ref[0, :, sl] * acc

        @pl.when(i == n - 1)
        def _():
            _peer_wait_slot(buf, sem, 1 - slot)

    step(i % 2)


def pack_expert_table(u, v):
    bits = lambda t: lax.bitcast_convert_type(t.astype(jnp.bfloat16), jnp.uint16).astype(jnp.uint32)
    return ((bits(u) << 16) | bits(v)).reshape(u.shape[0], PEER_DTILES, LANES)


def peer_experts(h, e, g, x, out_gate, uv, rows_per_batch, interpret=False):
    n, d = h.shape
    steps = n // PEER_TOK
    idx = e.reshape(steps, 1, PEER_ROWS)
    gate = g.reshape(steps, 1, PEER_ROWS)
    return pl.pallas_call(
        _peer_expert_kernel,
        grid=(steps,),
        in_specs=[
            pl.BlockSpec((1, 1, PEER_ROWS), lambda i: (0, 0, 0), memory_space=pltpu.SMEM),
            pl.BlockSpec((1, 1, PEER_ROWS), lambda i: (jnp.minimum(i + 1, steps - 1), 0, 0),
                         memory_space=pltpu.SMEM),
            pl.BlockSpec((PEER_TOK, d), lambda i: (i, 0)),
            pl.BlockSpec((1, 1, PEER_ROWS), lambda i: (i, 0, 0)),
            pl.BlockSpec((PEER_TOK, d), lambda i: (i, 0)),
            _per_batch_spec(out_gate, rows_per_batch, PEER_TOK),
            pl.BlockSpec(memory_space=pl.ANY),
        ],
        out_specs=pl.BlockSpec((PEER_TOK, d), lambda i: (i, 0)),
        out_shape=jax.ShapeDtypeStruct((n, d), jnp.float32),
        scratch_shapes=[pltpu.VMEM((2, PEER_GROUPS, PEER_DTILES, SUBLANES, LANES), jnp.uint32),
                        pltpu.SemaphoreType.DMA((2,))],
        compiler_params=pltpu.CompilerParams(
            dimension_semantics=("arbitrary",),
            vmem_limit_bytes=40 * 1024 * 1024),
        name="peer_experts",
        interpret=interpret,
    )(idx, idx, h, gate, x, out_gate, uv)


ROUTE_TOK = 128


def _top_rounds(s, rounds):
    n_rows, t = s.shape
    row = lax.broadcasted_iota(jnp.int32, (n_rows, t), 0)
    out_row = lax.broadcasted_iota(jnp.int32, (rounds, t), 0)
    vals = jnp.zeros((rounds, t), jnp.float32)
    idxs = jnp.zeros((rounds, t), jnp.int32)
    for r in range(rounds):
        m = jnp.max(s, axis=0, keepdims=True)
        i = jnp.min(jnp.where(s == m, row, n_rows), axis=0, keepdims=True)
        vals = jnp.where(out_row == r, m, vals)
        idxs = jnp.where(out_row == r, i, idxs)
        s = jnp.where(row == i, -jnp.inf, s)
    return vals, idxs


def _top_pair_sums(x, y):
    k, t = x.shape
    assert k == 16
    sub = lax.broadcasted_iota(jnp.int32, (SUBLANES, t), 0)
    blocks = [x[0:1, :] + y]
    for a in range(1, 8):
        blocks.append(jnp.where(sub < k // (a + 1), x[a:a + 1, :] + y[0:SUBLANES, :], -jnp.inf))
    blocks.append(x[8:16, :] + y[0:1, :])
    vals, r = _top_rounds(jnp.concatenate(blocks, axis=0), k)
    blk = r // SUBLANES
    a_idx = jnp.where(r < k, 0, jnp.where(blk < 9, blk - 1, r - 64))
    b_idx = jnp.where(r < k, r, jnp.where(blk < 9, r % SUBLANES, 0))
    return vals, a_idx, b_idx


def _select_rows(table, sel):
    k, t = table.shape
    out = jnp.zeros(sel.shape, table.dtype)
    for j in range(k):
        out = jnp.where(sel == j, table[j:j + 1, :], out)
    return out


def _peer_route_kernel(x_ref, nw_ref, shift_ref, scale_ref, wq_ref, keys_ref, h_ref, e_ref, g_ref):
    x = x_ref[...]
    h = x * lax.rsqrt(jnp.mean(x * x, -1, keepdims=True) + NORM_EPS) * nw_ref[...]
    h = h * (1 + scale_ref[0]) + shift_ref[0]
    h_ref[...] = h
    q = _dot(h, wq_ref[...])
    e_rows, g_rows = [], []
    for hd in range(PEER_HEADS):
        sv, si = [], []
        for p in range(2):
            c0 = (hd * 2 + p) * PEER_HALF
            s_t = _dot_nt(keys_ref[hd, p], q[:, c0:c0 + PEER_HALF])
            v_, i_ = _top_rounds(s_t, PEER_TOPK)
            sv.append(v_)
            si.append(i_)
        fv, fa, fb = _top_pair_sums(sv[0], sv[1])
        i1 = _select_rows(si[0], fa)
        i2 = _select_rows(si[1], fb)
        e_rows.append(i1 * PEER_NKEYS + i2)
        ex = jnp.exp(fv - jnp.max(fv, axis=0, keepdims=True))
        g_rows.append(ex / jnp.sum(ex, axis=0, keepdims=True))
    e_ref[...] = jnp.concatenate(e_rows, axis=0).T
    g_ref[...] = jnp.concatenate(g_rows, axis=0).T


def _per_batch_spec(vec, rows_per_batch, tile):
    if vec.shape[0] == 1:
        return pl.BlockSpec((1, 1, vec.shape[2]), lambda i: (0, 0, 0))
    return pl.BlockSpec((1, 1, vec.shape[2]), lambda i: (i // (rows_per_batch // tile), 0, 0))


def peer_route(x, norm_w, shift, scale, wq, keys, rows_per_batch, interpret=False):
    n, d = x.shape
    return pl.pallas_call(
        _peer_route_kernel,
        grid=(n // ROUTE_TOK,),
        in_specs=[pl.BlockSpec((ROUTE_TOK, d), lambda i: (i, 0)),
                  pl.BlockSpec((1, d), lambda i: (0, 0)),
                  _per_batch_spec(shift, rows_per_batch, ROUTE_TOK),
                  _per_batch_spec(scale, rows_per_batch, ROUTE_TOK),
                  pl.BlockSpec(wq.shape, lambda i: (0, 0)),
                  pl.BlockSpec(keys.shape, lambda i: (0, 0, 0, 0))],
        out_specs=[pl.BlockSpec((ROUTE_TOK, d), lambda i: (i, 0)),
                   pl.BlockSpec((ROUTE_TOK, PEER_SEL), lambda i: (i, 0)),
                   pl.BlockSpec((ROUTE_TOK, PEER_SEL), lambda i: (i, 0))],
        out_shape=[jax.ShapeDtypeStruct((n, d), jnp.float32),
                   jax.ShapeDtypeStruct((n, PEER_SEL), jnp.int32),
                   jax.ShapeDtypeStruct((n, PEER_SEL), jnp.float32)],
        compiler_params=pltpu.CompilerParams(
            dimension_semantics=("parallel",), vmem_limit_bytes=40 * 1024 * 1024),
        name="peer_route",
        interpret=interpret,
    )(x, norm_w, shift, scale, wq, keys)


def rmsnorm(x, g):
    xf = x.astype(jnp.float32)
    y = xf * lax.rsqrt(jnp.mean(xf * xf, -1, keepdims=True) + NORM_EPS)
    return y.astype(x.dtype) * g


def modulate(x, g, shift, scale):
    return rmsnorm(x, g) * (1 + scale) + shift


def shift_grid(h):
    b_, l, d = h.shape
    rows = l // GRID_W
    g = h.reshape(b_, rows, GRID_W, d)
    q = d // 4
    left = jnp.pad(g[:, :, :-1, :q], ((0, 0), (0, 0), (1, 0), (0, 0)))
    right = jnp.pad(g[:, :, 1:, q:2 * q], ((0, 0), (0, 0), (0, 1), (0, 0)))
    up = jnp.pad(g[:, :-1, :, 2 * q:3 * q], ((0, 0), (1, 0), (0, 0), (0, 0)))
    down = jnp.pad(g[:, 1:, :, 3 * q:], ((0, 0), (0, 1), (0, 0), (0, 0)))
    return jnp.concatenate([left, right, up, down], -1).reshape(b_, l, d)


def shift_seq(h):
    d = h.shape[-1] // 2
    prev = jnp.pad(h[:, :-1, :d], ((0, 0), (1, 0), (0, 0)))
    nxt = jnp.pad(h[:, 1:, d:], ((0, 0), (0, 1), (0, 0)))
    return jnp.concatenate([prev, nxt], -1)


def rwkv_mixer(hc, hl, need_ctx, p):
    (mu, w_rkv, w0, w1, w2, a0, a1, a2, g1, g2, k_k, k_a, r_k, ln_w, ln_b, w_o) = p
    f32 = jnp.float32

    def run(h, shifted, s_init):
        b_, l, d = h.shape
        heads = lambda t: t.reshape(b_, l, RW_HEADS, RW_HEAD).astype(f32)
        xx = shifted - h
        rkv = jnp.einsum('jbld,jde->jble', h[None] + xx[None] * mu[:3, None, None, :], w_rkv)
        r, k, v = rkv[0].astype(f32), rkv[1], rkv[2].astype(f32)
        xw, xa, xg = h + xx * mu[3], h + xx * mu[4], h + xx * mu[5]
        kk = heads(k * k_k)
        kk = (kk * lax.rsqrt(jnp.sum(kk * kk, -1, keepdims=True) + 1e-12)).reshape(b_, l, d)
        y = jnp.zeros_like(v)
        k_sum = jnp.zeros_like(v)
        finals = []
        for dr in range(2):
            wlog = -jax.nn.softplus(-(w0[dr] + jnp.tanh(xw @ w1[dr]) @ w2[dr]).astype(f32)) - 0.5
            a = jax.nn.sigmoid((a0[dr] + (xa @ a1[dr]) @ a2[dr]).astype(f32))
            kd = (k * (1 + (a - 1) * k_a)).astype(f32)
            ys, sf = rwkv_chunked(r, -jnp.exp(wlog), kd, v, kk, kk * a, s_init[dr], dr == 1)
            y = y + ys
            k_sum = k_sum + kd
            finals.append(sf)
        return (heads(y), heads(r), heads(v), heads(k_sum), xg), finals

    def post(h, y, r, v, k_sum, xg):
        b_, l, d = h.shape
        mean = jnp.mean(y, -1, keepdims=True)
        var = jnp.mean(jnp.square(y - mean), -1, keepdims=True)
        yn = ((y - mean) * lax.rsqrt(var + RW_GN_EPS)).reshape(b_, l, d) * ln_w + ln_b
        bonus = jnp.sum(r * k_sum * r_k.reshape(RW_HEADS, RW_HEAD).astype(f32), -1, keepdims=True) * v
        g = jax.nn.sigmoid(xg @ g1) @ g2
        return ((yn + bonus.reshape(b_, l, d)).astype(h.dtype) * g) @ w_o

    s_zero = jnp.zeros((hc.shape[0], RW_HEADS, RW_HEAD, RW_HEAD), f32)
    out_c, s_ctx = run(hc, shift_seq(hc), (s_zero, s_zero))
    out_l, _ = run(hl, shift_grid(hl), s_ctx)
    yc = post(hc, *out_c) if need_ctx else None
    return yc, post(hl, *out_l)


def _gla_chunk_kernel(qf_ref, kf_ref, vf_ref, ldf_ref, qb_ref, kb_ref, vb_ref, ldb_ref, s0f_ref, s0b_ref,
                      of_ref, ob_ref, sff_ref, sfb_ref, sf_scr, sb_scr):
    f32 = jnp.float32
    C = GLA_CHUNK
    c_idx = pl.program_id(2)
    n_chunks = pl.num_programs(2)

    @pl.when(c_idx == 0)
    def _():
        sf_scr[...] = s0f_ref[0, 0]
        sb_scr[...] = s0b_ref[0, 0]

    row = lax.broadcasted_iota(jnp.int32, (C, C), 0)
    col = lax.broadcasted_iota(jnp.int32, (C, C), 1)
    incls = [row >= col, row <= col]
    qs = [qf_ref[0], qb_ref[0]]
    ks = [kf_ref[0], kb_ref[0]]
    vs = [vf_ref[0], vb_ref[0]]
    lds = [ldf_ref[0], ldb_ref[0]]
    states = [sf_scr[...], sb_scr[...]]
    cums = [_dot(jnp.where(m, 1.0, 0.0).astype(f32), ld, _HI) for m, ld in zip(incls, lds)]
    mids = [cum[C // 2:C // 2 + 1, :] for cum in cums]
    lasts = [cums[0][C - 1:C, :], cums[1][0:1, :]]
    scores = [jnp.where(m, _dot_nt(q * jnp.exp(cum - mid), k * jnp.exp(mid - cum)), 0.0)
              for m, q, k, cum, mid in zip(incls, qs, ks, cums, mids)]
    outs = [_dot(sc, v) + _dot(q * jnp.exp(cum), s)
            for sc, v, q, cum, s in zip(scores, vs, qs, cums, states)]
    decays = [jnp.transpose(jnp.broadcast_to(jnp.exp(last), (GLA_DK, GLA_DK))) for last in lasts]
    new_states = [jnp.concatenate([dec] * (GLA_DV // GLA_DK), axis=1) * s + _dot_tn(k * jnp.exp(last - cum), v)
                  for dec, s, k, last, cum, v in zip(decays, states, ks, lasts, cums, vs)]
    of_ref[0] = outs[0]
    ob_ref[0] = outs[1]
    sf_scr[...] = new_states[0]
    sb_scr[...] = new_states[1]

    @pl.when(c_idx == n_chunks - 1)
    def _():
        sff_ref[0, 0] = new_states[0]
        sfb_ref[0, 0] = new_states[1]


def gla_scan(q, k, v, ld_f, ld_b, s0_f, s0_b, interpret=False):
    b_, l, _ = q.shape
    n = l // GLA_CHUNK
    fwd = lambda b, h, c: (b, c, h)
    bwd = lambda b, h, c: (b, n - 1 - c, h)
    kspec = lambda m: pl.BlockSpec((1, GLA_CHUNK, GLA_DK), m)
    vspec = lambda m: pl.BlockSpec((1, GLA_CHUNK, GLA_DV), m)
    sspec = pl.BlockSpec((1, 1, GLA_DK, GLA_DV), lambda b, h, c: (b, h, 0, 0))
    o_f, o_b, sf, sb = pl.pallas_call(
        _gla_chunk_kernel,
        grid=(b_, GLA_HEADS, n),
        in_specs=[kspec(fwd), kspec(fwd), vspec(fwd), kspec(fwd),
                  kspec(bwd), kspec(bwd), vspec(bwd), kspec(bwd), sspec, sspec],
        out_specs=[vspec(fwd), vspec(bwd), sspec, sspec],
        out_shape=[jax.ShapeDtypeStruct(v.shape, jnp.float32), jax.ShapeDtypeStruct(v.shape, jnp.float32),
                   jax.ShapeDtypeStruct(s0_f.shape, jnp.float32), jax.ShapeDtypeStruct(s0_b.shape, jnp.float32)],
        scratch_shapes=[pltpu.VMEM((GLA_DK, GLA_DV), jnp.float32), pltpu.VMEM((GLA_DK, GLA_DV), jnp.float32)],
        compiler_params=pltpu.CompilerParams(
            dimension_semantics=("parallel", "parallel", "arbitrary")),
        name="gla_scan",
        interpret=interpret,
    )(q, k, v, ld_f, q, k, v, ld_b, s0_f, s0_b)
    return o_f + o_b, sf, sb


def gla_mixer(hc, hl, need_ctx, p):
    w_in, wg1, wg2, bg, norm_w, w_o = p
    f32 = jnp.float32
    hk = GLA_HEADS * GLA_DK

    def run(h, s_init):
        b_, l, d = h.shape
        q, k, v, og = jnp.split(h @ w_in, [hk, 2 * hk, 2 * hk + d], -1)
        q = q.astype(f32) * GLA_DK ** -0.5
        logd = [jax.nn.log_sigmoid(((h @ wg1[dr]) @ wg2[dr] + bg[dr]).astype(f32)) / GLA_GATE_NORM
                for dr in range(2)]
        o, s_f, s_b = gla_scan(q, k.astype(f32), v.astype(f32), logd[0], logd[1], s_init[0], s_init[1])
        return o.reshape(b_, l, GLA_HEADS, GLA_DV), og, (s_f, s_b)

    def post(h, o, og):
        b_, l, d = h.shape
        on = o * lax.rsqrt(jnp.mean(o * o, -1, keepdims=True) + NORM_EPS) * norm_w
        return (on.reshape(b_, l, d).astype(h.dtype) * jax.nn.silu(og)) @ w_o

    s_zero = jnp.zeros((hc.shape[0], GLA_HEADS, GLA_DK, GLA_DV), f32)
    oc, ogc, s_ctx = run(hc, (s_zero, s_zero))
    ol, ogl, _ = run(hl, s_ctx)
    yc = post(hc, oc, ogc) if need_ctx else None
    return yc, post(hl, ol, ogl)


def conv3(z, w, b):
    zp = jnp.pad(z, ((0, 0), (1, 1), (0, 0)))
    return zp[:, :-2] * w[0] + zp[:, 1:-1] * w[1] + zp[:, 2:] * w[2] + b


def hyena_filters(l, f1, fb1, f2, fb2, f3, fb3, f4, freq):
    f32 = jnp.float32
    t = jnp.linspace(0.0, 1.0, l, dtype=f32)[:, None]
    w = 2 * math.pi * jnp.arange(l, dtype=f32)[:, None] / l
    f = jnp.linspace(1e-4, HY_BANDS - 1, HY_BANDS, dtype=f32)[None]
    z = jnp.concatenate([t, jnp.cos(f * w), -jnp.sin(f * w)], -1)
    fr = freq.astype(f32)
    hdn = jnp.sin(fr * (z @ f1.astype(f32) + fb1.astype(f32)))
    hdn = jnp.sin(fr * (hdn @ f2.astype(f32) + fb2.astype(f32)))
    hdn = jnp.sin(fr * (hdn @ f3.astype(f32) + fb3.astype(f32)))
    filt = (hdn @ f4.astype(f32)).reshape(l, HY_ORDER, 2, D_MODEL)
    min_decay = math.log(HY_TARGET) / HY_SLOW_DECAY
    max_decay = math.log(HY_TARGET) / HY_FAST_DECAY
    deltas = jnp.linspace(min_decay, max_decay, D_MODEL, dtype=f32)
    filt = filt * jnp.exp(-t * jnp.abs(deltas))[:, None, None, :]
    fwd, bwd = filt[:, :, 0], filt[:, :, 1]
    kc = jnp.concatenate([fwd, jnp.zeros_like(fwd[:1]), jnp.flip(bwd[1:], 0)], 0)
    return kc / jnp.sum(jnp.abs(kc), 0, keepdims=True)


def long_conv(u, kc_o, bias_o):
    b_, l, _ = u.shape
    half = b_ // 2
    z = lax.complex(u[:half], u[half:])
    kf = jnp.fft.fft(kc_o.astype(jnp.complex64), n=2 * l, axis=0)
    out = jnp.fft.ifft(jnp.fft.fft(z, n=2 * l, axis=1) * kf[None], n=2 * l, axis=1)[:, :l]
    return jnp.concatenate([jnp.real(out), jnp.imag(out)], axis=0) + u * bias_o


def hyena_mixer(h, p):
    (w_in, b_in, sc_w, sc_b, f1, fb1, f2, fb2, f3, fb3, f4, freq, bias, w_o, b_o) = p
    l = h.shape[1]
    z = conv3(h @ w_in + b_in, sc_w, sc_b).astype(jnp.float32)
    v, x1, x2 = jnp.split(z, 3, -1)
    kc = hyena_filters(l, f1, fb1, f2, fb2, f3, fb3, f4, freq)
    bias = bias.astype(jnp.float32)
    y = x1 * long_conv(v, kc[:, 0], bias[0])
    y = x2 * long_conv(y, kc[:, 1], bias[1])
    return y.astype(h.dtype) @ w_o + b_o


def peer_layer(x, norm_w, shift, scale, out_gate, p):
    wq, keys, uv = p
    b_, l, d = x.shape
    xf = x.reshape(b_ * l, d)
    h, e, g = peer_route(xf, norm_w.reshape(1, d), shift, scale, wq, keys, l)
    return peer_experts(h, e, g, xf, out_gate, uv, l).reshape(b_, l, d)


def kernel(x, c, ctx, c_ctx, ada_w, ada_b, norm_mix, norm_ffn, norm_out,
           rw_mu, rw_wrkv, rw_w0, rw_w1, rw_w2, rw_a0, rw_a1, rw_a2, rw_g1, rw_g2,
           rw_kk, rw_ka, rw_rk, rw_lnw, rw_lnb, rw_wo,
           gla_win, gla_wg1, gla_wg2, gla_bg, gla_normw, gla_wo,
           hy_win, hy_bin, hy_scw, hy_scb, hy_f1, hy_fb1, hy_f2, hy_fb2, hy_f3, hy_fb3,
           hy_f4, hy_freq, hy_bias, hy_wo, hy_bo,
           pe_wq, pe_keys, pe_u, pe_v):
    rw = lambda j: (rw_mu[j], rw_wrkv[j], rw_w0[j], rw_w1[j], rw_w2[j], rw_a0[j], rw_a1[j], rw_a2[j],
                    rw_g1[j], rw_g2[j], rw_kk[j], rw_ka[j], rw_rk[j], rw_lnw[j], rw_lnb[j], rw_wo[j])
    gla = lambda j: (gla_win[j], gla_wg1[j], gla_wg2[j], gla_bg[j], gla_normw[j], gla_wo[j])
    hy = lambda j: (hy_win[j], hy_bin[j], hy_scw[j], hy_scb[j], hy_f1[j], hy_fb1[j], hy_f2[j], hy_fb2[j],
                    hy_f3[j], hy_fb3[j], hy_f4[j], hy_freq[j], hy_bias[j], hy_wo[j], hy_bo[j])
    pe_uv = [pack_expert_table(pe_u[i], pe_v[i]) for i in range(DEPTH)]
    pe = lambda i: (pe_wq[i], pe_keys[i], pe_uv[i])

    xl, xc = x, ctx
    cond = jax.nn.silu(c)
    cond_ctx = jax.nn.silu(c_ctx)
    for i in range(DEPTH):
        kind, j = i % N_MIXERS, i // N_MIXERS
        last = i == DEPTH - 1
        mod = (cond @ ada_w[i] + ada_b[i]).reshape(-1, 1, 6, D_MODEL)
        mod_c = (cond_ctx @ ada_w[i] + ada_b[i]).reshape(1, 1, 6, D_MODEL)
        hl = modulate(xl, norm_mix[i], mod[:, :, 0], mod[:, :, 1])
        if kind == 2:
            yl = hyena_mixer(hl, hy(j))
            yc = None if last else hyena_mixer(modulate(xc, norm_mix[i], mod_c[:, :, 0], mod_c[:, :, 1]), hy(j))
        else:
            hc = modulate(xc, norm_mix[i], mod_c[:, :, 0], mod_c[:, :, 1])
            if kind == 0:
                yc, yl = rwkv_mixer(hc, hl, not last, rw(j))
            else:
                yc, yl = gla_mixer(hc, hl, not last, gla(j))
        xl = xl + mod[:, :, 2] * yl
        xl = peer_layer(xl, norm_ffn[i], mod[:, :, 3], mod[:, :, 4], mod[:, :, 5], pe(i))
        if not last:
            xc = xc + mod_c[:, :, 2] * yc
            xc = peer_layer(xc, norm_ffn[i], mod_c[:, :, 3], mod_c[:, :, 4], mod_c[:, :, 5], pe(i))
    return _final_norm(xl, norm_out)
```

```python
import math
import functools
import jax, jax.numpy as jnp
from jax import lax
from jax.experimental import pallas as pl
from jax.experimental.pallas import tpu as pltpu

D_MODEL = 1024
BATCH = 8
SEQ = 8192
DEPTH = 4

GRID_W = 64
CTX_LEN = 256
N_MIXERS = 3
NORM_EPS = 1e-6

RW_HEAD = 64
RW_HEADS = D_MODEL // RW_HEAD
RW_GN_EPS = 64e-5

GLA_HEADS = 4
GLA_DK = D_MODEL // 2 // GLA_HEADS
GLA_DV = D_MODEL // GLA_HEADS
GLA_GATE_NORM = 16.0
GLA_CHUNK = 64

HY_ORDER = 2
HY_EMB = 33
HY_BANDS = (HY_EMB - 1) // 2
HY_FAST_DECAY = 0.3
HY_SLOW_DECAY = 1.5
HY_TARGET = 1e-2

PEER_HEADS = 8
PEER_NKEYS = 128
PEER_QDIM = 256
PEER_HALF = PEER_QDIM // 2
PEER_TOPK = 16
PEER_BLOCK = 128


def _final_norm_kernel(x_ref, g_ref, o_ref):
    x = x_ref[...]
    y = x * lax.rsqrt(jnp.mean(x * x, -1, keepdims=True) + NORM_EPS)
    o_ref[...] = y * g_ref[...]


def _final_norm(x, g):
    b_, l, d = x.shape
    rows = b_ * l
    tile = 1024
    out = pl.pallas_call(
        _final_norm_kernel,
        grid=(rows // tile,),
        in_specs=[pl.BlockSpec((tile, d), lambda i: (i, 0)),
                  pl.BlockSpec((1, d), lambda i: (0, 0))],
        out_specs=pl.BlockSpec((tile, d), lambda i: (i, 0)),
        out_shape=jax.ShapeDtypeStruct((rows, d), x.dtype),
        name="final_norm",
    )(x.reshape(rows, d), g.reshape(1, d))
    return out.reshape(b_, l, d)


RW_CHUNK = 64
RW_LANES = 128
RW_PAIRS_PER_STEP = 8
_HI = lax.Precision.HIGHEST


def _dot(x, y, precision=None):
    return jnp.dot(x, y, precision=precision, preferred_element_type=jnp.float32)


def _dot_bf16(x, y):
    return _dot(x.astype(jnp.bfloat16), y.astype(jnp.bfloat16))


def _dot_nt(x, y, precision=None):
    return lax.dot_general(x, y, (((1,), (1,)), ((), ())), precision=precision,
                           preferred_element_type=jnp.float32)


def _dot_tn(x, y, precision=None):
    return lax.dot_general(x, y, (((0,), (0,)), ((), ())), precision=precision,
                           preferred_element_type=jnp.float32)


RW_INV_BASE = 8


def _unit_triangular_inverses(mats, upper):
    c = mats[0].shape[0]
    row = lax.broadcasted_iota(jnp.int32, (c, c), 0)
    col = lax.broadcasted_iota(jnp.int32, (c, c), 1)
    ps = [jnp.where(row // RW_INV_BASE == col // RW_INV_BASE, a, 0.0) for a in mats]
    ts = [jnp.where(row == col, 1.0, 0.0) + p for p in ps]
    b = 2
    while b < RW_INV_BASE:
        ps = [_dot_bf16(p, p) for p in ps]
        ts = [t + _dot_bf16(t, p) for t, p in zip(ts, ps)]
        b *= 2
    b = RW_INV_BASE
    while b < c:
        rb, cb = row // b, col // b
        coupling = (row // (2 * b) == col // (2 * b)) & ((rb + 1 == cb) if upper else (rb == cb + 1))
        ws = [_dot_bf16(jnp.where(coupling, a, 0.0), t) for a, t in zip(mats, ts)]
        ts = [t + _dot_bf16(t, w) for t, w in zip(ts, ws)]
        b *= 2
    return ts


def _rwkv_chunk_pairs(seqs, ms, reverse):
    f32 = jnp.float32
    C = RW_CHUNK
    row = lax.broadcasted_iota(jnp.int32, (C, C), 0)
    col = lax.broadcasted_iota(jnp.int32, (C, C), 1)
    if reverse:
        incl, strict = row <= col, row < col
    else:
        incl, strict = row >= col, row > col
    tri = jnp.where(incl, 1.0, 0.0).astype(f32)
    lane = lax.broadcasted_iota(jnp.int32, (1, RW_LANES), 1)
    head_masks = [jnp.where(lane < RW_HEAD, 1.0, 0.0).astype(f32),
                  jnp.where(lane >= RW_HEAD, 1.0, 0.0).astype(f32)]
    ri = lax.broadcasted_iota(jnp.int32, (RW_LANES, RW_LANES), 0)
    ci = lax.broadcasted_iota(jnp.int32, (RW_LANES, RW_LANES), 1)
    same_head = (ri < RW_HEAD) == (ci < RW_HEAD)
    n = len(seqs)

    cums = [_dot(tri, s[1], _HI) for s in seqs]
    pre = []
    for (r, lw, k, v, kk, bb), cum in zip(seqs, cums):
        mid = cum[C // 2:C // 2 + 1, :]
        tot = cum[0:1, :] if reverse else cum[C - 1:C, :]
        e_inv = jnp.exp(mid - cum)
        e_out = jnp.exp(tot - cum)
        pre.append(dict(
            tot=tot, v=v,
            a_rel=-kk * jnp.exp(cum - lw - mid), b_rel=bb * e_inv, k_rel=k * e_inv,
            r_rel=r * jnp.exp(cum - mid), a_abs=-kk * jnp.exp(cum - lw), r_abs=r * jnp.exp(cum),
            k_out=k * e_out, b_out=bb * e_out))
    uns = [_dot(p["a_abs"], m) for p, m in zip(pre, ms)]
    yns = [_dot(p["r_abs"], m) for p, m in zip(pre, ms)]
    a_hs = [[p["a_rel"] * mh for mh in head_masks] for p in pre]
    a_abs_ = [[jnp.where(strict, _dot_nt(a_h, p["b_rel"]), 0.0) for a_h in a_hs[i]] for i, p in enumerate(pre)]
    a_aks = [[jnp.where(strict, _dot_nt(a_h, p["k_rel"]), 0.0) for a_h in a_hs[i]] for i, p in enumerate(pre)]
    rhss = [[uns[i] + _dot(a_ak, p["v"]) for a_ak in a_aks[i]] for i, p in enumerate(pre)]
    invs = _unit_triangular_inverses([a for pair in a_abs_ for a in pair], upper=reverse)
    us = []
    for i in range(n):
        parts = [_dot(invs[2 * i + h], rhss[i][h], _HI) for h in range(2)]
        us.append(head_masks[0] * parts[0] + head_masks[1] * parts[1])
    r_hs = [[p["r_rel"] * mh for mh in head_masks] for p in pre]
    r_bs = [[jnp.where(incl, _dot_nt(r_h, p["b_rel"]), 0.0) for r_h in r_hs[i]] for i, p in enumerate(pre)]
    r_ks = [[jnp.where(incl, _dot_nt(r_h, p["k_rel"]), 0.0) for r_h in r_hs[i]] for i, p in enumerate(pre)]
    out = []
    for i, p in enumerate(pre):
        y = yns[i]
        for h, mh in enumerate(head_masks):
            y = y + mh * (_dot(r_bs[i][h], us[i]) + _dot(r_ks[i][h], p["v"]))
        decay = jnp.transpose(jnp.broadcast_to(jnp.exp(p["tot"]), (RW_LANES, RW_LANES)))
        m_new = decay * ms[i] + jnp.where(
            same_head, _dot_tn(p["b_out"], us[i]) + _dot_tn(p["k_out"], p["v"]), 0.0)
        out.append((y, m_new))
    return out


def _rwkv_chunk_kernel(r_ref, lw_ref, k_ref, v_ref, kk_ref, bb_ref, m0_ref,
                       y_ref, mfin_ref, m_scr, *, reverse):
    c_idx = pl.program_id(2)
    n_chunks = pl.num_programs(2)

    @pl.when(c_idx == 0)
    def _():
        m_scr[...] = m0_ref[0]

    lanes = [slice(p * RW_LANES, (p + 1) * RW_LANES) for p in range(RW_PAIRS_PER_STEP)]
    seqs = [tuple(ref[0, :, sl] for ref in (r_ref, lw_ref, k_ref, v_ref, kk_ref, bb_ref)) for sl in lanes]
    results = _rwkv_chunk_pairs(seqs, [m_scr[p] for p in range(RW_PAIRS_PER_STEP)], reverse)
    for p, (sl, (y, m_new)) in enumerate(zip(lanes, results)):
        y_ref[0, :, sl] = y
        m_scr[p] = m_new

    @pl.when(c_idx == n_chunks - 1)
    def _():
        mfin_ref[0] = m_scr[...]


def rwkv_chunked(r, lw, k, v, kk, bb, s0, reverse, interpret=False):
    b_, l, d = r.shape
    n_pairs = d // RW_LANES
    n_chunks = l // RW_CHUNK
    hn = RW_HEAD
    st = jnp.swapaxes(s0, -1, -2).reshape(b_, n_pairs, 2, hn, hn)
    zero = jnp.zeros_like(st[:, :, 0])
    m0 = jnp.concatenate([jnp.concatenate([st[:, :, 0], zero], -1),
                          jnp.concatenate([zero, st[:, :, 1]], -1)], -2)
    if reverse:
        seq_map = lambda b, p, c: (b, n_chunks - 1 - c, p)
    else:
        seq_map = lambda b, p, c: (b, c, p)
    pps = RW_PAIRS_PER_STEP
    seq_spec = pl.BlockSpec((1, RW_CHUNK, pps * RW_LANES), seq_map)
    st_spec = pl.BlockSpec((1, pps, RW_LANES, RW_LANES), lambda b, p, c: (b, p, 0, 0))
    y, mfin = pl.pallas_call(
        functools.partial(_rwkv_chunk_kernel, reverse=reverse),
        grid=(b_, n_pairs // pps, n_chunks),
        in_specs=[seq_spec] * 6 + [st_spec],
        out_specs=[seq_spec, st_spec],
        out_shape=[jax.ShapeDtypeStruct((b_, l, d), jnp.float32),
                   jax.ShapeDtypeStruct((b_, n_pairs, RW_LANES, RW_LANES), jnp.float32)],
        scratch_shapes=[pltpu.VMEM((pps, RW_LANES, RW_LANES), jnp.float32)],
        compiler_params=pltpu.CompilerParams(
            dimension_semantics=("parallel", "parallel", "arbitrary")),
        name="rwkv_scan_bwd" if reverse else "rwkv_scan_fwd",
        interpret=interpret,
    )(r, lw, k, v, kk, bb, m0)
    s_fin = jnp.stack([mfin[:, :, :hn, :hn], mfin[:, :, hn:, hn:]], 2)
    return y, jnp.swapaxes(s_fin.reshape(b_, 2 * n_pairs, hn, hn), -1, -2)


PEER_TOK = 16
PEER_SEL = PEER_HEADS * PEER_TOPK
PEER_ROWS = PEER_TOK * PEER_SEL
SUBLANES = 8
LANES = 128
PEER_GROUPS = PEER_ROWS // SUBLANES
PEER_DTILES = D_MODEL // LANES
PEER_RB = 128


def _peer_issue_group(idx_ref, uv_hbm, buf, sem, slot, grp):
    for sub in range(SUBLANES):
        e = idx_ref[0, 0, grp * SUBLANES + sub]
        pltpu.make_async_copy(uv_hbm.at[e], buf.at[slot, grp, :, sub, :],
                              sem.at[slot]).start(priority=sub % 2)


def _peer_wait_slot(buf, sem, slot):
    pltpu.make_async_copy(buf.at[1 - slot], buf.at[slot], sem.at[slot]).wait()


def _peer_expert_kernel(idx_first_ref, idx_next_ref, h_ref, g_ref, x_ref, og_ref, uv_hbm, o_ref, buf, sem):
    i = pl.program_id(0)
    n = pl.num_programs(0)

    @pl.when(i == 0)
    def _():
        def body(grp, carry):
            _peer_issue_group(idx_first_ref, uv_hbm, buf, sem, 0, grp)
            return carry
        lax.fori_loop(0, PEER_GROUPS, body, 0)

    bf16 = jnp.bfloat16
    n_rb = PEER_ROWS // PEER_RB
    grp_rb = PEER_RB // SUBLANES
    pieces = 2 * PEER_DTILES * n_rb
    per_piece = PEER_GROUPS // pieces

    def step(slot):
        _peer_wait_slot(buf, sem, slot)

        def issue_piece(piece):
            for grp in range(piece * per_piece, (piece + 1) * per_piece):
                _peer_issue_group(idx_next_ref, uv_hbm, buf, sem, 1 - slot, grp)

        def half(rb, c, low):
            x = buf[slot, rb * grp_rb:(rb + 1) * grp_rb, c, :, :].reshape(PEER_RB, LANES)
            x = (x << 16) if low else (x & jnp.uint32(0xFFFF0000))
            return lax.bitcast_convert_type(x, jnp.float32).astype(bf16)

        h = h_ref[...].astype(bf16)
        piece = 0
        blocks = []
        for rb in range(n_rb):
            sc = jnp.zeros((PEER_TOK, PEER_RB), jnp.float32)
            for c in range(PEER_DTILES):
                sc = sc + _dot_nt(h[:, c * LANES:(c + 1) * LANES], half(rb, c, False))
                issue_piece(piece)
                piece += 1
            blocks.append(sc)
        scores = jnp.concatenate(blocks, axis=1)
        tok = lax.broadcasted_iota(jnp.int32, (PEER_TOK, PEER_ROWS), 0)
        owner = lax.broadcasted_iota(jnp.int32, (PEER_TOK, PEER_ROWS), 1) // PEER_SEL
        own = tok == owner
        s_row = jnp.sum(jnp.where(own, scores, 0.0), axis=0, keepdims=True)
        act = 0.5 * s_row * (1.0 + lax.erf(s_row * (2.0 ** -0.5))) * g_ref[0]
        a_mat = jnp.where(own, jnp.broadcast_to(act, (PEER_TOK, PEER_ROWS)), 0.0).astype(bf16)
        for c in range(PEER_DTILES):
            acc = jnp.zeros((PEER_TOK, LANES), jnp.float32)
            for rb in range(n_rb):
                acc = acc + _dot(a_mat[:, rb * PEER_RB:(rb + 1) * PEER_RB], half(rb, c, True))
                issue_piece(piece)
                piece += 1
            sl = slice(c * LANES, (c + 1) * LANES)
            o_ref[:, sl] = x_ref[:, sl] + og_ref[0, :, sl] * acc

        @pl.when(i == n - 1)
        def _():
            _peer_wait_slot(buf, sem, 1 - slot)

    step(i % 2)


def pack_expert_table(u, v):
    bits = lambda t: lax.bitcast_convert_type(t.astype(jnp.bfloat16), jnp.uint16).astype(jnp.uint32)
    return ((bits(u) << 16) | bits(v)).reshape(u.shape[0], PEER_DTILES, LANES)


def peer_experts(h, e, g, x, out_gate, uv, rows_per_batch, interpret=False):
    n, d = h.shape
    steps = n // PEER_TOK
    idx = e.reshape(steps, 1, PEER_ROWS)
    gate = g.reshape(steps, 1, PEER_ROWS)
    return pl.pallas_call(
        _peer_expert_kernel,
        grid=(steps,),
        in_specs=[
            pl.BlockSpec((1, 1, PEER_ROWS), lambda i: (0, 0, 0), memory_space=pltpu.SMEM),
            pl.BlockSpec((1, 1, PEER_ROWS), lambda i: (jnp.minimum(i + 1, steps - 1), 0, 0),
                         memory_space=pltpu.SMEM),
            pl.BlockSpec((PEER_TOK, d), lambda i: (i, 0)),
            pl.BlockSpec((1, 1, PEER_ROWS), lambda i: (i, 0, 0)),
            pl.BlockSpec((PEER_TOK, d), lambda i: (i, 0)),
            _per_batch_spec(out_gate, rows_per_batch, PEER_TOK),
            pl.BlockSpec(memory_space=pl.ANY),
        ],
        out_specs=pl.BlockSpec((PEER_TOK, d), lambda i: (i, 0)),
        out_shape=jax.ShapeDtypeStruct((n, d), jnp.float32),
        scratch_shapes=[pltpu.VMEM((2, PEER_GROUPS, PEER_DTILES, SUBLANES, LANES), jnp.uint32),
                        pltpu.SemaphoreType.DMA((2,))],
        compiler_params=pltpu.CompilerParams(
            dimension_semantics=("arbitrary",),
            vmem_limit_bytes=40 * 1024 * 1024),
        name="peer_experts",
        interpret=interpret,
    )(idx, idx, h, gate, x, out_gate, uv)


ROUTE_TOK = 128


def _top_rounds(s, rounds):
    n_rows, t = s.shape
    row = lax.broadcasted_iota(jnp.int32, (n_rows, t), 0)
    out_row = lax.broadcasted_iota(jnp.int32, (rounds, t), 0)
    vals = jnp.zeros((rounds, t), jnp.float32)
    idxs = jnp.zeros((rounds, t), jnp.int32)
    for r in range(rounds):
        m = jnp.max(s, axis=0, keepdims=True)
        i = jnp.min(jnp.where(s == m, row, n_rows), axis=0, keepdims=True)
        vals = jnp.where(out_row == r, m, vals)
        idxs = jnp.where(out_row == r, i, idxs)
        s = jnp.where(row == i, -jnp.inf, s)
    return vals, idxs


def _top_pair_sums(x, y):
    k, t = x.shape
    assert k == 16
    sub = lax.broadcasted_iota(jnp.int32, (SUBLANES, t), 0)
    blocks = [x[0:1, :] + y]
    for a in range(1, 8):
        blocks.append(jnp.where(sub < k // (a + 1), x[a:a + 1, :] + y[0:SUBLANES, :], -jnp.inf))
    blocks.append(x[8:16, :] + y[0:1, :])
    vals, r = _top_rounds(jnp.concatenate(blocks, axis=0), k)
    blk = r // SUBLANES
    a_idx = jnp.where(r < k, 0, jnp.where(blk < 9, blk - 1, r - 64))
    b_idx = jnp.where(r < k, r, jnp.where(blk < 9, r % SUBLANES, 0))
    return vals, a_idx, b_idx


def _select_rows(table, sel):
    k, t = table.shape
    out = jnp.zeros(sel.shape, table.dtype)
    for j in range(k):
        out = jnp.where(sel == j, table[j:j + 1, :], out)
    return out


def _peer_route_kernel(x_ref, nw_ref, shift_ref, scale_ref, wq_ref, keys_ref, h_ref, e_ref, g_ref):
    x = x_ref[...]
    h = x * lax.rsqrt(jnp.mean(x * x, -1, keepdims=True) + NORM_EPS) * nw_ref[...]
    h = h * (1 + scale_ref[0]) + shift_ref[0]
    h_ref[...] = h
    q = _dot(h, wq_ref[...])
    e_rows, g_rows = [], []
    for hd in range(PEER_HEADS):
        sv, si = [], []
        for p in range(2):
            c0 = (hd * 2 + p) * PEER_HALF
            s_t = _dot_nt(keys_ref[hd, p], q[:, c0:c0 + PEER_HALF])
            v_, i_ = _top_rounds(s_t, PEER_TOPK)
            sv.append(v_)
            si.append(i_)
        fv, fa, fb = _top_pair_sums(sv[0], sv[1])
        i1 = _select_rows(si[0], fa)
        i2 = _select_rows(si[1], fb)
        e_rows.append(i1 * PEER_NKEYS + i2)
        ex = jnp.exp(fv - jnp.max(fv, axis=0, keepdims=True))
        g_rows.append(ex / jnp.sum(ex, axis=0, keepdims=True))
    e_ref[...] = jnp.concatenate(e_rows, axis=0).T
    g_ref[...] = jnp.concatenate(g_rows, axis=0).T


def _per_batch_spec(vec, rows_per_batch, tile):
    if vec.shape[0] == 1:
        return pl.BlockSpec((1, 1, vec.shape[2]), lambda i: (0, 0, 0))
    return pl.BlockSpec((1, 1, vec.shape[2]), lambda i: (i // (rows_per_batch // tile), 0, 0))


def peer_route(x, norm_w, shift, scale, wq, keys, rows_per_batch, interpret=False):
    n, d = x.shape
    return pl.pallas_call(
        _peer_route_kernel,
        grid=(n // ROUTE_TOK,),
        in_specs=[pl.BlockSpec((ROUTE_TOK, d), lambda i: (i, 0)),
                  pl.BlockSpec((1, d), lambda i: (0, 0)),
                  _per_batch_spec(shift, rows_per_batch, ROUTE_TOK),
                  _per_batch_spec(scale, rows_per_batch, ROUTE_TOK),
                  pl.BlockSpec(wq.shape, lambda i: (0, 0)),
                  pl.BlockSpec(keys.shape, lambda i: (0, 0, 0, 0))],
        out_specs=[pl.BlockSpec((ROUTE_TOK, d), lambda i: (i, 0)),
                   pl.BlockSpec((ROUTE_TOK, PEER_SEL), lambda i: (i, 0)),
                   pl.BlockSpec((ROUTE_TOK, PEER_SEL), lambda i: (i, 0))],
        out_shape=[jax.ShapeDtypeStruct((n, d), jnp.float32),
                   jax.ShapeDtypeStruct((n, PEER_SEL), jnp.int32),
                   jax.ShapeDtypeStruct((n, PEER_SEL), jnp.float32)],
        compiler_params=pltpu.CompilerParams(
            dimension_semantics=("parallel",), vmem_limit_bytes=40 * 1024 * 1024),
        name="peer_route",
        interpret=interpret,
    )(x, norm_w, shift, scale, wq, keys)


def rmsnorm(x, g):
    xf = x.astype(jnp.float32)
    y = xf * lax.rsqrt(jnp.mean(xf * xf, -1, keepdims=True) + NORM_EPS)
    return y.astype(x.dtype) * g


def modulate(x, g, shift, scale):
    return rmsnorm(x, g) * (1 + scale) + shift


def shift_grid(h):
    b_, l, d = h.shape
    rows = l // GRID_W
    g = h.reshape(b_, rows, GRID_W, d)
    q = d // 4
    left = jnp.pad(g[:, :, :-1, :q], ((0, 0), (0, 0), (1, 0), (0, 0)))
    right = jnp.pad(g[:, :, 1:, q:2 * q], ((0, 0), (0, 0), (0, 1), (0, 0)))
    up = jnp.pad(g[:, :-1, :, 2 * q:3 * q], ((0, 0), (1, 0), (0, 0), (0, 0)))
    down = jnp.pad(g[:, 1:, :, 3 * q:], ((0, 0), (0, 1), (0, 0), (0, 0)))
    return jnp.concatenate([left, right, up, down], -1).reshape(b_, l, d)


def shift_seq(h):
    d = h.shape[-1] // 2
    prev = jnp.pad(h[:, :-1, :d], ((0, 0), (1, 0), (0, 0)))
    nxt = jnp.pad(h[:, 1:, d:], ((0, 0), (0, 1), (0, 0)))
    return jnp.concatenate([prev, nxt], -1)


def rwkv_mixer(hc, hl, need_ctx, p):
    (mu, w_rkv, w0, w1, w2, a0, a1, a2, g1, g2, k_k, k_a, r_k, ln_w, ln_b, w_o) = p
    f32 = jnp.float32

    def run(h, shifted, s_init):
        b_, l, d = h.shape
        heads = lambda t: t.reshape(b_, l, RW_HEADS, RW_HEAD).astype(f32)
        xx = shifted - h
        rkv = jnp.einsum('jbld,jde->jble', h[None] + xx[None] * mu[:3, None, None, :], w_rkv)
        r, k, v = rkv[0].astype(f32), rkv[1], rkv[2].astype(f32)
        xw, xa, xg = h + xx * mu[3], h + xx * mu[4], h + xx * mu[5]
        kk = heads(k * k_k)
        kk = (kk * lax.rsqrt(jnp.sum(kk * kk, -1, keepdims=True) + 1e-12)).reshape(b_, l, d)
        y = jnp.zeros_like(v)
        k_sum = jnp.zeros_like(v)
        finals = []
        for dr in range(2):
            wlog = -jax.nn.softplus(-(w0[dr] + jnp.tanh(xw @ w1[dr]) @ w2[dr]).astype(f32)) - 0.5
            a = jax.nn.sigmoid((a0[dr] + (xa @ a1[dr]) @ a2[dr]).astype(f32))
            kd = (k * (1 + (a - 1) * k_a)).astype(f32)
            ys, sf = rwkv_chunked(r, -jnp.exp(wlog), kd, v, kk, kk * a, s_init[dr], dr == 1)
            y = y + ys
            k_sum = k_sum + kd
            finals.append(sf)
        return (heads(y), heads(r), heads(v), heads(k_sum), xg), finals

    def post(h, y, r, v, k_sum, xg):
        b_, l, d = h.shape
        mean = jnp.mean(y, -1, keepdims=True)
        var = jnp.mean(jnp.square(y - mean), -1, keepdims=True)
        yn = ((y - mean) * lax.rsqrt(var + RW_GN_EPS)).reshape(b_, l, d) * ln_w + ln_b
        bonus = jnp.sum(r * k_sum * r_k.reshape(RW_HEADS, RW_HEAD).astype(f32), -1, keepdims=True) * v
        g = jax.nn.sigmoid(xg @ g1) @ g2
        return ((yn + bonus.reshape(b_, l, d)).astype(h.dtype) * g) @ w_o

    s_zero = jnp.zeros((hc.shape[0], RW_HEADS, RW_HEAD, RW_HEAD), f32)
    out_c, s_ctx = run(hc, shift_seq(hc), (s_zero, s_zero))
    out_l, _ = run(hl, shift_grid(hl), s_ctx)
    yc = post(hc, *out_c) if need_ctx else None
    return yc, post(hl, *out_l)


def _gla_chunk_kernel(qf_ref, kf_ref, vf_ref, ldf_ref, qb_ref, kb_ref, vb_ref, ldb_ref, s0f_ref, s0b_ref,
                      of_ref, ob_ref, sff_ref, sfb_ref, sf_scr, sb_scr):
    f32 = jnp.float32
    C = GLA_CHUNK
    c_idx = pl.program_id(2)
    n_chunks = pl.num_programs(2)

    @pl.when(c_idx == 0)
    def _():
        sf_scr[...] = s0f_ref[0, 0]
        sb_scr[...] = s0b_ref[0, 0]

    row = lax.broadcasted_iota(jnp.int32, (C, C), 0)
    col = lax.broadcasted_iota(jnp.int32, (C, C), 1)
    incls = [row >= col, row <= col]
    qs = [qf_ref[0], qb_ref[0]]
    ks = [kf_ref[0], kb_ref[0]]
    vs = [vf_ref[0], vb_ref[0]]
    lds = [ldf_ref[0], ldb_ref[0]]
    states = [sf_scr[...], sb_scr[...]]
    cums = [_dot(jnp.where(m, 1.0, 0.0).astype(f32), ld, _HI) for m, ld in zip(incls, lds)]
    mids = [cum[C // 2:C // 2 + 1, :] for cum in cums]
    lasts = [cums[0][C - 1:C, :], cums[1][0:1, :]]
    scores = [jnp.where(m, _dot_nt(q * jnp.exp(cum - mid), k * jnp.exp(mid - cum)), 0.0)
              for m, q, k, cum, mid in zip(incls, qs, ks, cums, mids)]
    outs = [_dot(sc, v) + _dot(q * jnp.exp(cum), s)
            for sc, v, q, cum, s in zip(scores, vs, qs, cums, states)]
    decays = [jnp.transpose(jnp.broadcast_to(jnp.exp(last), (GLA_DK, GLA_DK))) for last in lasts]
    new_states = [jnp.concatenate([dec] * (GLA_DV // GLA_DK), axis=1) * s + _dot_tn(k * jnp.exp(last - cum), v)
                  for dec, s, k, last, cum, v in zip(decays, states, ks, lasts, cums, vs)]
    of_ref[0] = outs[0]
    ob_ref[0] = outs[1]
    sf_scr[...] = new_states[0]
    sb_scr[...] = new_states[1]

    @pl.when(c_idx == n_chunks - 1)
    def _():
        sff_ref[0, 0] = new_states[0]
        sfb_ref[0, 0] = new_states[1]


def gla_scan(q, k, v, ld_f, ld_b, s0_f, s0_b, interpret=False):
    b_, l, _ = q.shape
    n = l // GLA_CHUNK
    fwd = lambda b, h, c: (b, c, h)
    bwd = lambda b, h, c: (b, n - 1 - c, h)
    kspec = lambda m: pl.BlockSpec((1, GLA_CHUNK, GLA_DK), m)
    vspec = lambda m: pl.BlockSpec((1, GLA_CHUNK, GLA_DV), m)
    sspec = pl.BlockSpec((1, 1, GLA_DK, GLA_DV), lambda b, h, c: (b, h, 0, 0))
    o_f, o_b, sf, sb = pl.pallas_call(
        _gla_chunk_kernel,
        grid=(b_, GLA_HEADS, n),
        in_specs=[kspec(fwd), kspec(fwd), vspec(fwd), kspec(fwd),
                  kspec(bwd), kspec(bwd), vspec(bwd), kspec(bwd), sspec, sspec],
        out_specs=[vspec(fwd), vspec(bwd), sspec, sspec],
        out_shape=[jax.ShapeDtypeStruct(v.shape, jnp.float32), jax.ShapeDtypeStruct(v.shape, jnp.float32),
                   jax.ShapeDtypeStruct(s0_f.shape, jnp.float32), jax.ShapeDtypeStruct(s0_b.shape, jnp.float32)],
        scratch_shapes=[pltpu.VMEM((GLA_DK, GLA_DV), jnp.float32), pltpu.VMEM((GLA_DK, GLA_DV), jnp.float32)],
        compiler_params=pltpu.CompilerParams(
            dimension_semantics=("parallel", "parallel", "arbitrary")),
        name="gla_scan",
        interpret=interpret,
    )(q, k, v, ld_f, q, k, v, ld_b, s0_f, s0_b)
    return o_f + o_b, sf, sb


def gla_mixer(hc, hl, need_ctx, p):
    w_in, wg1, wg2, bg, norm_w, w_o = p
    f32 = jnp.float32
    hk = GLA_HEADS * GLA_DK

    def run(h, s_init):
        b_, l, d = h.shape
        q, k, v, og = jnp.split(h @ w_in, [hk, 2 * hk, 2 * hk + d], -1)
        q = q.astype(f32) * GLA_DK ** -0.5
        logd = [jax.nn.log_sigmoid(((h @ wg1[dr]) @ wg2[dr] + bg[dr]).astype(f32)) / GLA_GATE_NORM
                for dr in range(2)]
        o, s_f, s_b = gla_scan(q, k.astype(f32), v.astype(f32), logd[0], logd[1], s_init[0], s_init[1])
        return o.reshape(b_, l, GLA_HEADS, GLA_DV), og, (s_f, s_b)

    def post(h, o, og):
        b_, l, d = h.shape
        on = o * lax.rsqrt(jnp.mean(o * o, -1, keepdims=True) + NORM_EPS) * norm_w
        return (on.reshape(b_, l, d).astype(h.dtype) * jax.nn.silu(og)) @ w_o

    s_zero = jnp.zeros((hc.shape[0], GLA_HEADS, GLA_DK, GLA_DV), f32)
    oc, ogc, s_ctx = run(hc, (s_zero, s_zero))
    ol, ogl, _ = run(hl, s_ctx)
    yc = post(hc, oc, ogc) if need_ctx else None
    return yc, post(hl, ol, ogl)


def conv3(z, w, b):
    zp = jnp.pad(z, ((0, 0), (1, 1), (0, 0)))
    return zp[:, :-2] * w[0] + zp[:, 1:-1] * w[1] + zp[:, 2:] * w[2] + b


def hyena_filters(l, f1, fb1, f2, fb2, f3, fb3, f4, freq):
    f32 = jnp.float32
    t = jnp.linspace(0.0, 1.0, l, dtype=f32)[:, None]
    w = 2 * math.pi * jnp.arange(l, dtype=f32)[:, None] / l
    f = jnp.linspace(1e-4, HY_BANDS - 1, HY_BANDS, dtype=f32)[None]
    z = jnp.concatenate([t, jnp.cos(f * w), -jnp.sin(f * w)], -1)
    fr = freq.astype(f32)
    hdn = jnp.sin(fr * (z @ f1.astype(f32) + fb1.astype(f32)))
    hdn = jnp.sin(fr * (hdn @ f2.astype(f32) + fb2.astype(f32)))
    hdn = jnp.sin(fr * (hdn @ f3.astype(f32) + fb3.astype(f32)))
    filt = (hdn @ f4.astype(f32)).reshape(l, HY_ORDER, 2, D_MODEL)
    min_decay = math.log(HY_TARGET) / HY_SLOW_DECAY
    max_decay = math.log(HY_TARGET) / HY_FAST_DECAY
    deltas = jnp.linspace(min_decay, max_decay, D_MODEL, dtype=f32)
    filt = filt * jnp.exp(-t * jnp.abs(deltas))[:, None, None, :]
    fwd, bwd = filt[:, :, 0], filt[:, :, 1]
    kc = jnp.concatenate([fwd, jnp.zeros_like(fwd[:1]), jnp.flip(bwd[1:], 0)], 0)
    return kc / jnp.sum(jnp.abs(kc), 0, keepdims=True)


def long_conv(u, kc_o, bias_o):
    b_, l, _ = u.shape
    half = b_ // 2
    z = lax.complex(u[:half], u[half:])
    kf = jnp.fft.fft(kc_o.astype(jnp.complex64), n=2 * l, axis=0)
    out = jnp.fft.ifft(jnp.fft.fft(z, n=2 * l, axis=1) * kf[None], n=2 * l, axis=1)[:, :l]
    return jnp.concatenate([jnp.real(out), jnp.imag(out)], axis=0) + u * bias_o


def hyena_mixer(h, p):
    (w_in, b_in, sc_w, sc_b, f1, fb1, f2, fb2, f3, fb3, f4, freq, bias, w_o, b_o) = p
    l = h.shape[1]
    z = conv3(h @ w_in + b_in, sc_w, sc_b).astype(jnp.float32)
    v, x1, x2 = jnp.split(z, 3, -1)
    kc = hyena_filters(l, f1, fb1, f2, fb2, f3, fb3, f4, freq)
    bias = bias.astype(jnp.float32)
    y = x1 * long_conv(v, kc[:, 0], bias[0])
    y = x2 * long_conv(y, kc[:, 1], bias[1])
    return y.astype(h.dtype) @ w_o + b_o


def peer_layer(x, norm_w, shift, scale, out_gate, p):
    wq, keys, uv = p
    b_, l, d = x.shape
    xf = x.reshape(b_ * l, d)
    h, e, g = peer_route(xf, norm_w.reshape(1, d), shift, scale, wq, keys, l)
    return peer_experts(h, e, g, xf, out_gate, uv, l).reshape(b_, l, d)


def kernel(x, c, ctx, c_ctx, ada_w, ada_b, norm_mix, norm_ffn, norm_out,
           rw_mu, rw_wrkv, rw_w0, rw_w1, rw_w2, rw_a0, rw_a1, rw_a2, rw_g1, rw_g2,
           rw_kk, rw_ka, rw_rk, rw_lnw, rw_lnb, rw_wo,
           gla_win, gla_wg1, gla_wg2, gla_bg, gla_normw, gla_wo,
           hy_win, hy_bin, hy_scw, hy_scb, hy_f1, hy_fb1, hy_f2, hy_fb2, hy_f3, hy_fb3,
           hy_f4, hy_freq, hy_bias, hy_wo, hy_bo,
           pe_wq, pe_keys, pe_u, pe_v):
    rw = lambda j: (rw_mu[j], rw_wrkv[j], rw_w0[j], rw_w1[j], rw_w2[j], rw_a0[j], rw_a1[j], rw_a2[j],
                    rw_g1[j], rw_g2[j], rw_kk[j], rw_ka[j], rw_rk[j], rw_lnw[j], rw_lnb[j], rw_wo[j])
    gla = lambda j: (gla_win[j], gla_wg1[j], gla_wg2[j], gla_bg[j], gla_normw[j], gla_wo[j])
    hy = lambda j: (hy_win[j], hy_bin[j], hy_scw[j], hy_scb[j], hy_f1[j], hy_fb1[j], hy_f2[j], hy_fb2[j],
                    hy_f3[j], hy_fb3[j], hy_f4[j], hy_freq[j], hy_bias[j], hy_wo[j], hy_bo[j])
    pe_uv = [pack_expert_table(pe_u[i], pe_v[i]) for i in range(DEPTH)]
    pe = lambda i: (pe_wq[i], pe_keys[i], pe_uv[i])

    xl, xc = x, ctx
    cond = jax.nn.silu(c)
    cond_ctx = jax.nn.silu(c_ctx)
    for i in range(DEPTH):
        kind, j = i % N_MIXERS, i // N_MIXERS
        last = i == DEPTH - 1
        mod = (cond @ ada_w[i] + ada_b[i]).reshape(-1, 1, 6, D_MODEL)
        mod_c = (cond_ctx @ ada_w[i] + ada_b[i]).reshape(1, 1, 6, D_MODEL)
        hl = modulate(xl, norm_mix[i], mod[:, :, 0], mod[:, :, 1])
        if kind == 2:
            yl = hyena_mixer(hl, hy(j))
            yc = None if last else hyena_mixer(modulate(xc, norm_mix[i], mod_c[:, :, 0], mod_c[:, :, 1]), hy(j))
        else:
            hc = modulate(xc, norm_mix[i], mod_c[:, :, 0], mod_c[:, :, 1])
            if kind == 0:
                yc, yl = rwkv_mixer(hc, hl, not last, rw(j))
            else:
                yc, yl = gla_mixer(hc, hl, not last, gla(j))
        xl = xl + mod[:, :, 2] * yl
        xl = peer_layer(xl, norm_ffn[i], mod[:, :, 3], mod[:, :, 4], mod[:, :, 5], pe(i))
        if not last:
            xc = xc + mod_c[:, :, 2] * yc
            xc = peer_layer(xc, norm_ffn[i], mod_c[:, :, 3], mod_c[:, :, 4], mod_c[:, :, 5], pe(i))
    return _final_norm(xl, norm_out)
```
